```python
import jax, jax.numpy as jnp
from jax import lax
import numpy as np

D_MODEL = 2048
BATCH = 8
SEQ = 2048
DEPTH = 2

CTX_LEN = 256
GRID_W = 64
N_BRANCH = 4
BRANCH_W = D_MODEL // N_BRANCH
GROUP_W = 128
N_GROUPS = BRANCH_W // GROUP_W
N_IN_SLICES = 8
IN_W = N_IN_SLICES * BRANCH_W
POOL_WINDOWS = (2, 4, 8, 16)
CHUNK = 128
CONV_W = 4
LRU_C = 8.0
NA_HEADS = N_GROUPS
NA_HEAD_DIM = GROUP_W
NA_WIN_R = 8
NA_WIN_C = 16
D_FF = -(-(8 * D_MODEL) // (3 * 256)) * 256
EPS = 1e-6
NEG_INF = -1e30

kernel_name = "hybrid_pool_sgu_rglru_natten_gated_merge"


def rms_norm(x, g):
    xf = x.astype(jnp.float32)
    y = xf * lax.rsqrt(jnp.mean(xf * xf, axis=-1, keepdims=True) + EPS)
    return (y * g.astype(jnp.float32)).astype(x.dtype)


def layer_norm(x, g):
    xf = x.astype(jnp.float32)
    xc = xf - jnp.mean(xf, axis=-1, keepdims=True)
    y = xc * lax.rsqrt(jnp.mean(xc * xc, axis=-1, keepdims=True) + EPS)
    return (y * g.astype(jnp.float32)).astype(x.dtype)


def adaln(cond, w_mod, b_mod):
    m = (jax.nn.silu(cond) @ w_mod + b_mod).reshape(cond.shape[:-1] + (6, D_MODEL))
    return tuple(jnp.expand_dims(m[..., i, :], -2) for i in range(6))


def modulate(xn, shift, scale):
    return xn * (1.0 + scale) + shift


def heads(z):
    return z.reshape(z.shape[0], z.shape[1], NA_HEADS, NA_HEAD_DIM)


def block_diag_linear(z, w, b):
    B, L, _ = z.shape
    y = jnp.einsum('blgc,gcd->blgd', z.reshape(B, L, N_GROUPS, GROUP_W), w)
    return y.reshape(B, L, BRANCH_W) + b


def pool_branch(z, w_pool, pool_scale):
    B, L, _ = z.shape
    zf = z.astype(jnp.float32)
    cs = jnp.concatenate([jnp.zeros((B, 1, BRANCH_W), jnp.float32), jnp.cumsum(zf, axis=1)], axis=1)
    t = jnp.arange(L)
    outs = []
    for gi, w in enumerate(POOL_WINDOWS):
        lo = jnp.clip(t - w // 2, 0, L)
        hi = jnp.clip(t + w // 2, 0, L)
        sl = slice(gi * GROUP_W, (gi + 1) * GROUP_W)
        csg = cs[..., sl]
        mean = (csg[:, hi] - csg[:, lo]) / (hi - lo).astype(jnp.float32)[None, :, None]
        outs.append(mean - zf[..., sl])
    pooled = jnp.concatenate(outs, axis=-1).astype(z.dtype)
    y = jnp.einsum('blgc,gcd->blgd', pooled.reshape(B, L, N_GROUPS, GROUP_W), w_pool)
    return y.reshape(B, L, BRANCH_W) * pool_scale


def spatial_gating(u, v, ln_g, w_sp, b_sp):
    B, L, _ = v.shape
    vn = layer_norm(v, ln_g).reshape(B, L // CHUNK, CHUNK, N_GROUPS, GROUP_W)
    mixed = jnp.einsum('gpq,bnqgc->bnpgc', w_sp, vn) + b_sp.T[None, None, :, :, None]
    return u * mixed.reshape(B, L, BRANCH_W)


def centred_depthwise_conv(z, w, b):
    L = z.shape[1]
    left = CONV_W // 2
    zp = jnp.pad(z, ((0, 0), (left, CONV_W - 1 - left), (0, 0)))
    out = zp[:, 0:L] * w[0] + b
    for k in range(1, CONV_W):
        out = out + zp[:, k:k + L] * w[k]
    return out


def _affine_combine(lhs, rhs):
    a_l, b_l = lhs
    a_r, b_r = rhs
    return a_l * a_r, a_r * b_l + b_r


def rg_lru(z, w_r, b_r, w_i, b_i, lam, h0, reverse):
    r = jax.nn.sigmoid(block_diag_linear(z, w_r, b_r).astype(jnp.float32))
    i = jax.nn.sigmoid(block_diag_linear(z, w_i, b_i).astype(jnp.float32))
    log_a = -LRU_C * r * jax.nn.softplus(-lam.astype(jnp.float32))
    a = jnp.exp(log_a)
    b = jnp.sqrt(-jnp.expm1(2.0 * log_a)) * (i * z.astype(jnp.float32))
    a_cum, b_cum = lax.associative_scan(_affine_combine, (a, b), axis=1, reverse=reverse)
    return a_cum * h0[:, None, :] + b_cum


def neighbourhood_attention(q, k, v, k_ctx, v_ctx, rpb):
    B, L, H, Dh = q.shape
    rows = L // GRID_W
    wr = min(NA_WIN_R, rows)
    r_idx = jnp.arange(rows)
    key_rows = jnp.clip(r_idx - wr // 2, 0, rows - wr)[:, None] + jnp.arange(wr)[None, :]
    col = jnp.arange(GRID_W)
    col_start = jnp.clip(col - NA_WIN_C // 2, 0, GRID_W - NA_WIN_C)
    col_ok = (col[None, :] >= col_start[:, None]) & (col[None, :] < col_start[:, None] + NA_WIN_C)
    qg = q.reshape(B, rows, GRID_W, H, Dh)
    kg = k.reshape(B, rows, GRID_W, H, Dh)[:, key_rows]
    vg = v.reshape(B, rows, GRID_W, H, Dh)[:, key_rows]
    scale = Dh ** -0.5
    s_loc = jnp.einsum('brqhd,brjkhd->brhqjk', qg, kg).astype(jnp.float32) * scale
    d_row = key_rows - r_idx[:, None] + (NA_WIN_R - 1)
    d_col = jnp.clip(col[None, :] - col[:, None] + (NA_WIN_C - 1), 0, 2 * NA_WIN_C - 2)
    bias = rpb[:, d_row[:, None, :, None], d_col[None, :, None, :]]
    s_loc = s_loc + jnp.moveaxis(bias, 0, 1)[None].astype(jnp.float32)
    s_loc = jnp.where(col_ok[:, None, :], s_loc, NEG_INF).reshape(B, rows, H, GRID_W, wr * GRID_W)
    s_ctx = jnp.einsum('brqhd,bkhd->brhqk', qg, k_ctx).astype(jnp.float32) * scale
    p = jax.nn.softmax(jnp.concatenate([s_loc, s_ctx], axis=-1), axis=-1).astype(q.dtype)
    p_loc = p[..., :wr * GRID_W].reshape(B, rows, H, GRID_W, wr, GRID_W)
    p_ctx = p[..., wr * GRID_W:]
    out = jnp.einsum('brhqjk,brjkhd->brqhd', p_loc, vg) + jnp.einsum('brhqk,bkhd->brqhd', p_ctx, v_ctx)
    return out.reshape(B, L, H * Dh)


def context_attention(q, k, v):
    B, C, H, Dh = q.shape
    s = jnp.einsum('bqhd,bkhd->bhqk', q, k).astype(jnp.float32) * (Dh ** -0.5)
    p = jax.nn.softmax(s, axis=-1).astype(q.dtype)
    return jnp.einsum('bhqk,bkhd->bqhd', p, v).reshape(B, C, H * Dh)


def merge_branches(h, ys, w_branch, w_gate, w_out):
    merged = jax.nn.sigmoid(h @ w_gate[:, 0]) * (ys[0] @ w_branch[0])
    for n in range(1, N_BRANCH):
        merged = merged + jax.nn.sigmoid(h @ w_gate[:, n]) * (ys[n] @ w_branch[n])
    return merged @ w_out


def swiglu(h, w_g, w_u, w_d):
    return (jax.nn.silu(h @ w_g) * (h @ w_u)) @ w_d


def setup_inputs(seed: int = 0) -> dict:
    key = jax.random.key(seed)
    ks = jax.random.split(key, 32)
    f32 = jnp.float32
    D = D_MODEL

    def nrm(k, shape, std):
        return jax.random.normal(k, shape, f32) * std

    a_c = jax.random.uniform(ks[21], (DEPTH, 2, BRANCH_W), f32, 0.9, 0.999)
    a_base = a_c ** (1.0 / LRU_C)
    return {
        "x": nrm(ks[0], (BATCH, SEQ, D), 1.0),
        "c": nrm(ks[1], (BATCH, D), 1.0),
        "ctx": nrm(ks[2], (BATCH, CTX_LEN, D), 1.0),
        "c_ctx": nrm(ks[3], (D,), 1.0),
        "w_mod": nrm(ks[4], (DEPTH, D, 6 * D), 0.5 * D ** -0.5),
        "b_mod": nrm(ks[5], (DEPTH, 6 * D), 0.02),
        "g_mix": 1.0 + nrm(ks[6], (DEPTH, D), 0.02),
        "g_ffn": 1.0 + nrm(ks[7], (DEPTH, D), 0.02),
        "g_final": 1.0 + nrm(ks[8], (D,), 0.02),
        "w_in": nrm(ks[9], (DEPTH, D, IN_W), D ** -0.5),
        "w_pool": nrm(ks[10], (DEPTH, N_GROUPS, GROUP_W, GROUP_W), GROUP_W ** -0.5),
        "pool_scale": 1.0 + nrm(ks[11], (DEPTH, BRANCH_W), 0.1),
        "gmlp_ln_g": 1.0 + nrm(ks[12], (DEPTH, BRANCH_W), 0.02),
        "w_sp": nrm(ks[13], (DEPTH, N_GROUPS, CHUNK, CHUNK), CHUNK ** -0.5),
        "b_sp": 1.0 + nrm(ks[14], (DEPTH, N_GROUPS, CHUNK), 0.1),
        "conv_w": nrm(ks[15], (DEPTH, CONV_W, BRANCH_W), CONV_W ** -0.5),
        "conv_b": nrm(ks[16], (DEPTH, BRANCH_W), 0.02),
        "w_rg": nrm(ks[17], (DEPTH, 2, N_GROUPS, GROUP_W, GROUP_W), GROUP_W ** -0.5),
        "b_rg": nrm(ks[18], (DEPTH, 2, BRANCH_W), 0.1),
        "w_ig": nrm(ks[19], (DEPTH, 2, N_GROUPS, GROUP_W, GROUP_W), GROUP_W ** -0.5),
        "b_ig": nrm(ks[20], (DEPTH, 2, BRANCH_W), 0.1),
        "lru_lam": jnp.log(a_base) - jnp.log1p(-a_base),
        "rpb": nrm(ks[22], (DEPTH, NA_HEADS, 2 * NA_WIN_R - 1, 2 * NA_WIN_C - 1), 0.1),
        "w_branch": nrm(ks[23], (DEPTH, N_BRANCH, BRANCH_W, D), BRANCH_W ** -0.5),
        "w_gate": nrm(ks[24], (DEPTH, D, N_BRANCH, D), D ** -0.5),
        "w_out": nrm(ks[25], (DEPTH, D, D), D ** -0.5),
        "w_ffn_gate": nrm(ks[26], (DEPTH, D, D_FF), D ** -0.5),
        "w_ffn_up": nrm(ks[27], (DEPTH, D, D_FF), D ** -0.5),
        "w_ffn_down": nrm(ks[28], (DEPTH, D_FF, D), D_FF ** -0.5),
    }


def reference(x, c, ctx, c_ctx, w_mod, b_mod, g_mix, g_ffn, g_final, w_in, w_pool, pool_scale,
              gmlp_ln_g, w_sp, b_sp, conv_w, conv_b, w_rg, b_rg, w_ig, b_ig, lru_lam, rpb,
              w_branch, w_gate, w_out, w_ffn_gate, w_ffn_up, w_ffn_down):
    B = x.shape[0]
    for l in range(DEPTH):
        last = l == DEPTH - 1
        sh1, sc1, gt1, sh2, sc2, gt2 = adaln(c, w_mod[l], b_mod[l])
        csh1, csc1, cgt1, csh2, csc2, cgt2 = adaln(c_ctx, w_mod[l], b_mod[l])

        hx = modulate(rms_norm(x, g_mix[l]), sh1, sc1)
        hc = modulate(rms_norm(ctx, g_mix[l]), csh1, csc1)
        x_pool, x_u, x_v, x_lru, x_lg, x_q, x_k, x_val = jnp.split(hx @ w_in[l], N_IN_SLICES, axis=-1)
        c_pool, c_u, c_v, c_lru, c_lg, c_q, c_k, c_val = jnp.split(hc @ w_in[l], N_IN_SLICES, axis=-1)

        k_ctx, v_ctx = heads(c_k), heads(c_val)
        c_conv = centred_depthwise_conv(c_lru, conv_w[l], conv_b[l])
        h_zero = jnp.zeros((B, BRANCH_W), jnp.float32)
        hc_f = rg_lru(c_conv, w_rg[l, 0], b_rg[l, 0], w_ig[l, 0], b_ig[l, 0], lru_lam[l, 0], h_zero, False)
        hc_b = rg_lru(c_conv, w_rg[l, 1], b_rg[l, 1], w_ig[l, 1], b_ig[l, 1], lru_lam[l, 1], h_zero, True)

        y_pool = pool_branch(x_pool, w_pool[l], pool_scale[l])
        y_sgu = spatial_gating(x_u, x_v, gmlp_ln_g[l], w_sp[l], b_sp[l])
        x_conv = centred_depthwise_conv(x_lru, conv_w[l], conv_b[l])
        hx_f = rg_lru(x_conv, w_rg[l, 0], b_rg[l, 0], w_ig[l, 0], b_ig[l, 0], lru_lam[l, 0], hc_f[:, -1], False)
        hx_b = rg_lru(x_conv, w_rg[l, 1], b_rg[l, 1], w_ig[l, 1], b_ig[l, 1], lru_lam[l, 1], hc_b[:, 0], True)
        y_lru = (hx_f + hx_b).astype(x.dtype) * jax.nn.gelu(x_lg)
        y_na = neighbourhood_attention(heads(x_q), heads(x_k), heads(x_val), k_ctx, v_ctx, rpb[l])
        x_mixed = x + gt1 * merge_branches(hx, [y_pool, y_sgu, y_lru, y_na], w_branch[l], w_gate[l], w_out[l])

        if not last:
            yc_pool = pool_branch(c_pool, w_pool[l], pool_scale[l])
            yc_sgu = spatial_gating(c_u, c_v, gmlp_ln_g[l], w_sp[l], b_sp[l])
            yc_lru = (hc_f + hc_b).astype(ctx.dtype) * jax.nn.gelu(c_lg)
            yc_na = context_attention(heads(c_q), k_ctx, v_ctx)
            ctx = ctx + cgt1 * merge_branches(hc, [yc_pool, yc_sgu, yc_lru, yc_na], w_branch[l], w_gate[l], w_out[l])
            hc2 = modulate(rms_norm(ctx, g_ffn[l]), csh2, csc2)
            ctx = ctx + cgt2 * swiglu(hc2, w_ffn_gate[l], w_ffn_up[l], w_ffn_down[l])

        hx2 = modulate(rms_norm(x_mixed, g_ffn[l]), sh2, sc2)
        x = x_mixed + gt2 * swiglu(hx2, w_ffn_gate[l], w_ffn_up[l], w_ffn_down[l])
    return rms_norm(x, g_final)
```

```python
import functools

import jax
import jax.numpy as jnp
from jax import lax
from jax.experimental import pallas as pl
from jax.experimental.pallas import tpu as pltpu

D_MODEL = 2048
BATCH = 8
SEQ = 2048
DEPTH = 2
CTX_LEN = 256
GRID_W = 64
N_BRANCH = 4
BRANCH_W = D_MODEL // N_BRANCH
GROUP_W = 128
N_GROUPS = BRANCH_W // GROUP_W
N_IN_SLICES = 8
IN_W = N_IN_SLICES * BRANCH_W
POOL_WINDOWS = (2, 4, 8, 16)
CHUNK = 128
CONV_W = 4
LRU_C = 8.0
NA_WIN_R = 8
NA_WIN_C = 16
D_FF = 5632
EPS = 1e-6
NEG_INF = -1e30

N_MOD = 6
COND_ROWS = 16
CTX_COND_ROW = BATCH
SUBLANES = 8
POOL_HALO = 16
CONV_HALO = 8
VMEM_LIMIT = 56 * 1024 * 1024

F32 = jnp.float32
BF16 = jnp.bfloat16

S_POOL, S_U, S_V, S_LRU, S_LG, S_Q, S_K, S_VAL = range(N_IN_SLICES)


def _params(*sem):
    return pltpu.CompilerParams(dimension_semantics=sem, vmem_limit_bytes=VMEM_LIMIT)


def _rms_mod_rows(src_ref, dst_ref, g, sc, sh, rows, chunk=64):
    def body(c, carry):
        r = pl.multiple_of(c * chunk, chunk)
        x = src_ref[pl.ds(r, chunk), :]
        ms = jnp.mean(x * x, axis=-1, keepdims=True)
        y = (x * lax.rsqrt(ms + EPS)) * g
        dst_ref[pl.ds(r, chunk), :] = (y * (1.0 + sc) + sh).astype(dst_ref.dtype)
        return carry
    lax.fori_loop(0, rows // chunk, body, 0)


def _mod_body(c_ref, w_ref, b_ref, o_ref):
    a = jax.nn.silu(c_ref[...]).astype(BF16)
    o_ref[...] = jnp.dot(a, w_ref[...].astype(BF16), preferred_element_type=F32) + b_ref[...]


def _adaln_all(cond, w_mod, b_mod):
    tn = 1024
    n = N_MOD * D_MODEL
    out = pl.pallas_call(
        _mod_body,
        grid=(DEPTH, n // tn),
        in_specs=[
            pl.BlockSpec((COND_ROWS, D_MODEL), lambda l, j: (0, 0)),
            pl.BlockSpec((None, D_MODEL, tn), lambda l, j: (l, 0, j)),
            pl.BlockSpec((None, 1, tn), lambda l, j: (l, 0, j)),
        ],
        out_specs=pl.BlockSpec((None, COND_ROWS, tn), lambda l, j: (l, 0, j)),
        out_shape=jax.ShapeDtypeStruct((DEPTH, COND_ROWS, n), F32),
        compiler_params=_params("parallel", "parallel"),
        name="adaln",
    )(cond, w_mod, b_mod.reshape(DEPTH, 1, n))
    return out.reshape(DEPTH * COND_ROWS * N_MOD, 1, D_MODEL)


def _mod_spec(layer, which, row_of_tile):
    def index(i, *_):
        return ((layer * COND_ROWS + row_of_tile(i)) * N_MOD + which, 0, 0)
    return pl.BlockSpec((None, 1, D_MODEL), index)


def _vec_spec(n):
    return pl.BlockSpec((1, n), lambda *_: (0, 0))


def _inproj_body(x_ref, sh_ref, sc_ref, g_ref, w_ref, p_ref, h_ref, *, tm):
    @pl.when(pl.program_id(1) == 0)
    def _():
        _rms_mod_rows(x_ref, h_ref, g_ref[...], sc_ref[...], sh_ref[...], tm)
    p_ref[...] = jnp.dot(h_ref[...], w_ref[...], preferred_element_type=F32)


def _inproj(x, mods, layer, row_of_tile, g, w, tm):
    rows = x.shape[0]
    tn = 512
    return pl.pallas_call(
        functools.partial(_inproj_body, tm=tm),
        grid=(rows // tm, IN_W // tn),
        in_specs=[
            pl.BlockSpec((tm, D_MODEL), lambda i, j: (i, 0)),
            _mod_spec(layer, 0, row_of_tile),
            _mod_spec(layer, 1, row_of_tile),
            _vec_spec(D_MODEL),
            pl.BlockSpec((D_MODEL, tn), lambda i, j: (0, j)),
        ],
        out_specs=[
            pl.BlockSpec((tm, tn), lambda i, j: (i, j)),
            pl.BlockSpec((tm, D_MODEL), lambda i, j: (i, 0)),
        ],
        out_shape=[
            jax.ShapeDtypeStruct((rows, IN_W), F32),
            jax.ShapeDtypeStruct((rows, D_MODEL), BF16),
        ],
        compiler_params=_params("parallel", "arbitrary"),
        name="inproj",
    )(x, mods, mods, g, w)


def _pool_body(z_ref, w_ref, s_ref, o_ref, zp_ref, *, L):
    halo = jnp.zeros((POOL_HALO, BRANCH_W), F32)
    zp_ref[0:POOL_HALO, :] = halo
    zp_ref[POOL_HALO + L:2 * POOL_HALO + L, :] = halo
    zp_ref[POOL_HALO:POOL_HALO + L, :] = z_ref[...]
    rc = CHUNK
    for c in range(L // rc):
        r0 = c * rc
        t = r0 + lax.broadcasted_iota(jnp.int32, (rc, GROUP_W), 0)
        for gi, win in enumerate(POOL_WINDOWS):
            half = win // 2
            sl = slice(gi * GROUP_W, (gi + 1) * GROUP_W)
            acc = zp_ref[POOL_HALO + r0 - half:POOL_HALO + r0 - half + rc, sl]
            for o in range(-half + 1, half):
                acc = acc + zp_ref[POOL_HALO + r0 + o:POOL_HALO + r0 + o + rc, sl]
            cnt = (jnp.minimum(t + half, L) - jnp.maximum(t - half, 0)).astype(F32)
            pooled = acc / cnt - z_ref[r0:r0 + rc, sl]
            y = jnp.dot(pooled.astype(BF16), w_ref[gi], preferred_element_type=F32)
            o_ref[r0:r0 + rc, sl] = (y * s_ref[:, sl]).astype(BF16)


def _pool(p, L, w_pool, pool_scale):
    nb = p.shape[0] // L
    return pl.pallas_call(
        functools.partial(_pool_body, L=L),
        grid=(nb,),
        in_specs=[
            pl.BlockSpec((L, BRANCH_W), lambda b: (b, S_POOL)),
            pl.BlockSpec((N_GROUPS, GROUP_W, GROUP_W), lambda b: (0, 0, 0)),
            _vec_spec(BRANCH_W),
        ],
        out_specs=pl.BlockSpec((L, BRANCH_W), lambda b: (b, 0)),
        out_shape=jax.ShapeDtypeStruct((p.shape[0], BRANCH_W), BF16),
        scratch_shapes=[pltpu.VMEM((L + 2 * POOL_HALO, BRANCH_W), F32)],
        compiler_params=_params("parallel"),
        name="pool",
    )(p, w_pool, pool_scale)


def _sgu_body(u_ref, v_ref, g_ref, w_ref, b_ref, o_ref, *, L):
    for n in range(L // CHUNK):
        r0 = n * CHUNK
        v = v_ref[r0:r0 + CHUNK, :]
        vc = v - jnp.mean(v, axis=-1, keepdims=True)
        var = jnp.mean(vc * vc, axis=-1, keepdims=True)
        vn = ((vc * lax.rsqrt(var + EPS)) * g_ref[...]).astype(BF16)
        for gi in range(N_GROUPS):
            sl = slice(gi * GROUP_W, (gi + 1) * GROUP_W)
            mixed = jnp.dot(w_ref[gi], vn[:, sl], preferred_element_type=F32) + b_ref[gi]
            o_ref[r0:r0 + CHUNK, sl] = (u_ref[r0:r0 + CHUNK, sl] * mixed).astype(BF16)


def _sgu(p, L, ln_g, w_sp, b_sp):
    nb = p.shape[0] // L
    return pl.pallas_call(
        functools.partial(_sgu_body, L=L),
        grid=(nb,),
        in_specs=[
            pl.BlockSpec((L, BRANCH_W), lambda b: (b, S_U)),
            pl.BlockSpec((L, BRANCH_W), lambda b: (b, S_V)),
            _vec_spec(BRANCH_W),
            pl.BlockSpec((N_GROUPS, CHUNK, CHUNK), lambda b: (0, 0, 0)),
            pl.BlockSpec((N_GROUPS, CHUNK, 1), lambda b: (0, 0, 0)),
        ],
        out_specs=pl.BlockSpec((L, BRANCH_W), lambda b: (b, 0)),
        out_shape=jax.ShapeDtypeStruct((p.shape[0], BRANCH_W), BF16),
        compiler_params=_params("parallel"),
        name="sgu",
    )(p, p, ln_g, w_sp, b_sp)


def _lru_scan(a_ref, b_ref, h0, nblk, reverse, emit):
    row = lax.broadcasted_iota(jnp.int32, (SUBLANES, BRANCH_W), 0)

    def body(jj, h):
        j = (nblk - 1 - jj) if reverse else jj
        r = pl.multiple_of(j * SUBLANES, SUBLANES)
        a = a_ref[pl.ds(r, SUBLANES), :]
        b = b_ref[pl.ds(r, SUBLANES), :]
        for s in (1, 2, 4):
            shift = (SUBLANES - s) if reverse else s
            a_s = pltpu.roll(a, shift, 0)
            b_s = pltpu.roll(b, shift, 0)
            m = (row < SUBLANES - s) if reverse else (row >= s)
            b = jnp.where(m, a * b_s + b, b)
            a = jnp.where(m, a * a_s, a)
        hh = a * h + b
        emit(r, hh)
        edge = hh[0:1, :] if reverse else hh[SUBLANES - 1:SUBLANES, :]
        return jnp.broadcast_to(edge, (SUBLANES, BRANCH_W))

    return lax.fori_loop(0, nblk, body, h0, unroll=4)


def _lru_body(z_ref, lg_ref, cw_ref, cb_ref, wr_ref, br_ref, wi_ref, bi_ref, lam_ref, h0_ref,
              y_ref, hl_ref, zp_ref, a_ref, b_ref, hs_ref, *, L):
    halo = jnp.zeros((CONV_HALO, BRANCH_W), F32)
    zp_ref[0:CONV_HALO, :] = halo
    zp_ref[CONV_HALO + L:2 * CONV_HALO + L, :] = halo
    zp_ref[CONV_HALO:CONV_HALO + L, :] = z_ref[...]
    left = CONV_W // 2
    rc = CHUNK
    nblk = L // SUBLANES

    def conv_chunk(r0):
        out = zp_ref[CONV_HALO + r0 - left:CONV_HALO + r0 - left + rc, :] * cw_ref[0:1, :] + cb_ref[...]
        for k in range(1, CONV_W):
            lo = CONV_HALO + r0 - left + k
            out = out + zp_ref[lo:lo + rc, :] * cw_ref[k:k + 1, :]
        return out

    for d in range(2):
        reverse = d == 1
        sp = jax.nn.softplus(-lam_ref[d])
        for c in range(L // rc):
            r0 = c * rc
            cv = conv_chunk(r0)
            cvb = cv.astype(BF16)
            for gi in range(N_GROUPS):
                sl = slice(gi * GROUP_W, (gi + 1) * GROUP_W)
                rg = jax.nn.sigmoid(
                    jnp.dot(cvb[:, sl], wr_ref[d, gi], preferred_element_type=F32) + br_ref[d][:, sl])
                ig = jax.nn.sigmoid(
                    jnp.dot(cvb[:, sl], wi_ref[d, gi], preferred_element_type=F32) + bi_ref[d][:, sl])
                log_a = (-LRU_C * rg) * sp[:, sl]
                a_ref[r0:r0 + rc, sl] = jnp.exp(log_a)
                th = jnp.tanh(log_a)
                b_ref[r0:r0 + rc, sl] = jnp.sqrt(-2.0 * th / (1.0 - th)) * (ig * cv[:, sl])
        h0 = jnp.broadcast_to(h0_ref[0, d:d + 1, :], (SUBLANES, BRANCH_W))
        if not reverse:
            def emit(r, hh):
                hs_ref[pl.ds(r, SUBLANES), :] = hh
        else:
            def emit(r, hh):
                tot = hs_ref[pl.ds(r, SUBLANES), :] + hh
                y_ref[pl.ds(r, SUBLANES), :] = (tot * jax.nn.gelu(lg_ref[pl.ds(r, SUBLANES), :])).astype(BF16)
        h_end = _lru_scan(a_ref, b_ref, h0, nblk, reverse, emit)
        hl_ref[0, d:d + 1, :] = h_end[0:1, :]


def _lru(p, L, conv_w, conv_b, w_rg, b_rg, w_ig, b_ig, lam, h0):
    nb = p.shape[0] // L
    full = lambda *shape: pl.BlockSpec(shape, lambda b: (0,) * len(shape))
    return pl.pallas_call(
        functools.partial(_lru_body, L=L),
        grid=(nb,),
        in_specs=[
            pl.BlockSpec((L, BRANCH_W), lambda b: (b, S_LRU)),
            pl.BlockSpec((L, BRANCH_W), lambda b: (b, S_LG)),
            full(CONV_W, BRANCH_W),
            full(1, BRANCH_W),
            full(2, N_GROUPS, GROUP_W, GROUP_W),
            full(2, 1, BRANCH_W),
            full(2, N_GROUPS, GROUP_W, GROUP_W),
            full(2, 1, BRANCH_W),
            full(2, 1, BRANCH_W),
            pl.BlockSpec((1, 2, BRANCH_W), lambda b: (b, 0, 0)),
        ],
        out_specs=[
            pl.BlockSpec((L, BRANCH_W), lambda b: (b, 0)),
            pl.BlockSpec((1, 2, BRANCH_W), lambda b: (b, 0, 0)),
        ],
        out_shape=[
            jax.ShapeDtypeStruct((p.shape[0], BRANCH_W), BF16),
            jax.ShapeDtypeStruct((nb, 2, BRANCH_W), F32),
        ],
        scratch_shapes=[
            pltpu.VMEM((L + 2 * CONV_HALO, BRANCH_W), F32),
            pltpu.VMEM((L, BRANCH_W), F32),
            pltpu.VMEM((L, BRANCH_W), F32),
            pltpu.VMEM((L, BRANCH_W), F32),
        ],
        compiler_params=_params("parallel"),
        name="lru",
    )(p, p, conv_w, conv_b, w_rg, b_rg, w_ig, b_ig, lam, h0)


_NT = (((1,), (1,)), ((), ()))


def _softmax_pv(scores, values):
    m = scores[0].max(axis=-1, keepdims=True)
    for s in scores[1:]:
        m = jnp.maximum(m, s.max(axis=-1, keepdims=True))
    es = [jnp.exp(s - m) for s in scores]
    tot = es[0].sum(axis=-1, keepdims=True)
    for e in es[1:]:
        tot = tot + e.sum(axis=-1, keepdims=True)
    out = None
    for e, v in zip(es, values):
        o = jnp.dot((e / tot).astype(BF16), v, preferred_element_type=F32)
        out = o if out is None else out + o
    return out


def _nattn_body(q_ref, k_ref, v_ref, kc_ref, vc_ref, t_ref, o_ref, *, L):
    rows = L // GRID_W
    win = NA_WIN_R * GRID_W
    scale = GROUP_W ** -0.5
    kcb = kc_ref[...].astype(BF16)
    vcb = vc_ref[...].astype(BF16)

    def body(r, carry):
        row0 = jnp.clip(r - NA_WIN_R // 2, 0, rows - NA_WIN_R)
        qs = pl.multiple_of(r * GRID_W, GRID_W)
        ks = pl.multiple_of(row0 * GRID_W, GRID_W)
        q = q_ref[pl.ds(qs, GRID_W), :].astype(BF16)
        kw = k_ref[pl.ds(ks, win), :].astype(BF16)
        vw = v_ref[pl.ds(ks, win), :].astype(BF16)
        s = lax.dot_general(q, kw, _NT, preferred_element_type=F32) * scale
        base = row0 - r + (NA_WIN_R - 1)
        bias = jnp.concatenate([t_ref[base + 2 * jj] for jj in range(NA_WIN_R // 2)], axis=1)
        s = jnp.where(bias > 0.5 * NEG_INF, s + bias, NEG_INF)
        sc = lax.dot_general(q, kcb, _NT, preferred_element_type=F32) * scale
        o_ref[pl.ds(qs, GRID_W), :] = _softmax_pv([s, sc], [vw, vcb]).astype(BF16)
        return carry

    lax.fori_loop(0, rows, body, 0)


def _nattn(p, pc, bias_pairs):
    hcol = BRANCH_W // GROUP_W
    return pl.pallas_call(
        functools.partial(_nattn_body, L=SEQ),
        grid=(BATCH, N_GROUPS),
        in_specs=[
            pl.BlockSpec((SEQ, GROUP_W), lambda b, h: (b, S_Q * hcol + h)),
            pl.BlockSpec((SEQ, GROUP_W), lambda b, h: (b, S_K * hcol + h)),
            pl.BlockSpec((SEQ, GROUP_W), lambda b, h: (b, S_VAL * hcol + h)),
            pl.BlockSpec((CTX_LEN, GROUP_W), lambda b, h: (b, S_K * hcol + h)),
            pl.BlockSpec((CTX_LEN, GROUP_W), lambda b, h: (b, S_VAL * hcol + h)),
            pl.BlockSpec((None, 2 * NA_WIN_R - 2, GRID_W, 2 * GRID_W), lambda b, h: (h, 0, 0, 0)),
        ],
        out_specs=pl.BlockSpec((SEQ, GROUP_W), lambda b, h: (b, h)),
        out_shape=jax.ShapeDtypeStruct((p.shape[0], BRANCH_W), BF16),
        compiler_params=_params("parallel", "parallel"),
        name="nattn",
    )(p, p, p, pc, pc, bias_pairs)


def _cattn_body(q_ref, k_ref, v_ref, o_ref):
    scale = GROUP_W ** -0.5
    s = lax.dot_general(q_ref[...].astype(BF16), k_ref[...].astype(BF16), _NT,
                        preferred_element_type=F32) * scale
    o_ref[...] = _softmax_pv([s], [v_ref[...].astype(BF16)]).astype(BF16)


def _cattn(pc):
    hcol = BRANCH_W // GROUP_W
    return pl.pallas_call(
        _cattn_body,
        grid=(BATCH, N_GROUPS),
        in_specs=[
            pl.BlockSpec((CTX_LEN, GROUP_W), lambda b, h: (b, S_Q * hcol + h)),
            pl.BlockSpec((CTX_LEN, GROUP_W), lambda b, h: (b, S_K * hcol + h)),
            pl.BlockSpec((CTX_LEN, GROUP_W), lambda b, h: (b, S_VAL * hcol + h)),
        ],
        out_specs=pl.BlockSpec((CTX_LEN, GROUP_W), lambda b, h: (b, h)),
        out_shape=jax.ShapeDtypeStruct((pc.shape[0], BRANCH_W), BF16),
        compiler_params=_params("parallel", "parallel"),
        name="cattn",
    )(pc, pc, pc)


def _bias_pairs(rpb):
    col = jnp.arange(GRID_W)
    col_start = jnp.clip(col - NA_WIN_C // 2, 0, GRID_W - NA_WIN_C)
    col_ok = (col[None, :] >= col_start[:, None]) & (col[None, :] < col_start[:, None] + NA_WIN_C)
    d_col = jnp.clip(col[None, :] - col[:, None] + (NA_WIN_C - 1), 0, 2 * NA_WIN_C - 2)
    tab = jnp.where(col_ok[None, None], rpb[:, :, d_col].astype(F32), NEG_INF)
    return jnp.concatenate([tab[:, :-1], tab[:, 1:]], axis=-1)


def _merge_body(h_ref, y0_ref, y1_ref, y2_ref, y3_ref, g0_ref, g1_ref, g2_ref, g3_ref, wb_ref, o_ref):
    h = h_ref[...]
    acc = None
    for n, (y_ref, g_ref) in enumerate(((y0_ref, g0_ref), (y1_ref, g1_ref), (y2_ref, g2_ref), (y3_ref, g3_ref))):
        gate = jax.nn.sigmoid(jnp.dot(h, g_ref[...], preferred_element_type=F32))
        term = gate * jnp.dot(y_ref[...], wb_ref[n], preferred_element_type=F32)
        acc = term if acc is None else acc + term
    o_ref[...] = acc.astype(BF16)


def _merge(h, ys, w_gate, w_branch, tm):
    rows = h.shape[0]
    tn = 256
    nj = D_MODEL // tn
    gate_spec = lambda n: pl.BlockSpec((D_MODEL, tn), lambda i, j: (0, n * nj + j))
    y_spec = pl.BlockSpec((tm, BRANCH_W), lambda i, j: (i, 0))
    return pl.pallas_call(
        _merge_body,
        grid=(rows // tm, nj),
        in_specs=[pl.BlockSpec((tm, D_MODEL), lambda i, j: (i, 0)), y_spec, y_spec, y_spec, y_spec,
                  gate_spec(0), gate_spec(1), gate_spec(2), gate_spec(3),
                  pl.BlockSpec((N_BRANCH, BRANCH_W, tn), lambda i, j: (0, 0, j))],
        out_specs=pl.BlockSpec((tm, tn), lambda i, j: (i, j)),
        out_shape=jax.ShapeDtypeStruct((rows, D_MODEL), BF16),
        compiler_params=_params("parallel", "arbitrary"),
        name="merge",
    )(h, *ys, w_gate, w_gate, w_gate, w_gate, w_branch)


def _outproj_body(m_ref, x_ref, w_ref, gt_ref, g_ref, sh_ref, sc_ref, xm_ref, h2_ref, *, tm):
    xm_ref[...] = x_ref[...] + gt_ref[...] * jnp.dot(m_ref[...], w_ref[...], preferred_element_type=F32)
    _rms_mod_rows(xm_ref, h2_ref, g_ref[...], sc_ref[...], sh_ref[...], tm)


def _outproj(merged, x, w_out, mods, layer, row_of_tile, g_ffn, tm):
    rows = x.shape[0]
    row_spec = pl.BlockSpec((tm, D_MODEL), lambda i: (i, 0))
    return pl.pallas_call(
        functools.partial(_outproj_body, tm=tm),
        grid=(rows // tm,),
        in_specs=[row_spec, row_spec,
                  pl.BlockSpec((D_MODEL, D_MODEL), lambda i: (0, 0)),
                  _mod_spec(layer, 2, row_of_tile),
                  _vec_spec(D_MODEL),
                  _mod_spec(layer, 3, row_of_tile),
                  _mod_spec(layer, 4, row_of_tile)],
        out_specs=[row_spec, row_spec],
        out_shape=[jax.ShapeDtypeStruct((rows, D_MODEL), F32),
                   jax.ShapeDtypeStruct((rows, D_MODEL), BF16)],
        compiler_params=_params("parallel"),
        name="outproj",
    )(merged, x, w_out, mods, g_ffn, mods, mods)


def _ffn_body(h_ref, wg_ref, wu_ref, wd_ref, xm_ref, gt_ref, gf_ref, o_ref, acc_ref, *, tm, final):
    f = pl.program_id(1)

    @pl.when(f == 0)
    def _():
        acc_ref[...] = jnp.zeros_like(acc_ref)

    h = h_ref[...]
    g = jnp.dot(h, wg_ref[...], preferred_element_type=F32)
    u = jnp.dot(h, wu_ref[...], preferred_element_type=F32)
    a = (jax.nn.silu(g) * u).astype(BF16)
    acc_ref[...] += jnp.dot(a, wd_ref[...], preferred_element_type=F32)

    @pl.when(f == pl.num_programs(1) - 1)
    def _():
        chunk = 64

        def body(c, carry):
            r = pl.multiple_of(c * chunk, chunk)
            y = xm_ref[pl.ds(r, chunk), :] + gt_ref[...] * acc_ref[pl.ds(r, chunk), :]
            if final:
                ms = jnp.mean(y * y, axis=-1, keepdims=True)
                y = (y * lax.rsqrt(ms + EPS)) * gf_ref[...]
            o_ref[pl.ds(r, chunk), :] = y
            return carry
        lax.fori_loop(0, tm // chunk, body, 0)


def _ffn(h2, xm, w_g, w_u, w_d, mods, layer, row_of_tile, g_final, final, tm):
    rows = xm.shape[0]
    tf = 512
    row_spec = pl.BlockSpec((tm, D_MODEL), lambda i, f: (i, 0))
    return pl.pallas_call(
        functools.partial(_ffn_body, tm=tm, final=final),
        grid=(rows // tm, D_FF // tf),
        in_specs=[row_spec,
                  pl.BlockSpec((D_MODEL, tf), lambda i, f: (0, f)),
                  pl.BlockSpec((D_MODEL, tf), lambda i, f: (0, f)),
                  pl.BlockSpec((tf, D_MODEL), lambda i, f: (f, 0)),
                  row_spec,
                  _mod_spec(layer, 5, row_of_tile),
                  _vec_spec(D_MODEL)],
        out_specs=row_spec,
        out_shape=jax.ShapeDtypeStruct((rows, D_MODEL), F32),
        scratch_shapes=[pltpu.VMEM((tm, D_MODEL), F32)],
        compiler_params=_params("parallel", "arbitrary"),
        name="ffn",
    )(h2, w_g, w_u, w_d, xm, mods, g_final)


def kernel(x, c, ctx, c_ctx, w_mod, b_mod, g_mix, g_ffn, g_final, w_in, w_pool, pool_scale, gmlp_ln_g, w_sp, b_sp, conv_w, conv_b, w_rg, b_rg, w_ig, b_ig, lru_lam, rpb, w_branch, w_gate, w_out, w_ffn_gate, w_ffn_up, w_ffn_down):
    B, L, D = x.shape
    C = ctx.shape[1]
    assert (B, L, D, C) == (BATCH, SEQ, D_MODEL, CTX_LEN)
    xs = x.reshape(B * L, D)
    cs = ctx.reshape(B * C, D)

    cond = jnp.concatenate([c, c_ctx[None, :], jnp.zeros((COND_ROWS - B - 1, D), F32)], axis=0)
    mods = _adaln_all(cond, w_mod, b_mod)

    tm_in, tm_merge, tm_out, tm_ffn = 1024, 1024, 512, 512
    x_row = lambda tm: (lambda i: i // (SEQ // tm))
    c_row = lambda tm: (lambda i: CTX_COND_ROW)
    gfin = g_final.reshape(1, D)

    for l in range(DEPTH):
        last = l == DEPTH - 1
        gm = g_mix[l].reshape(1, D)
        gf = g_ffn[l].reshape(1, D)
        w_in_l = w_in[l].astype(BF16)
        w_pool_l = w_pool[l].astype(BF16)
        ps_l = pool_scale[l].reshape(1, BRANCH_W)
        lng_l = gmlp_ln_g[l].reshape(1, BRANCH_W)
        w_sp_l = w_sp[l].astype(BF16)
        b_sp_l = b_sp[l].reshape(N_GROUPS, CHUNK, 1)
        cb_l = conv_b[l].reshape(1, BRANCH_W)
        w_rg_l = w_rg[l].astype(BF16)
        w_ig_l = w_ig[l].astype(BF16)
        b_rg_l = b_rg[l].reshape(2, 1, BRANCH_W)
        b_ig_l = b_ig[l].reshape(2, 1, BRANCH_W)
        lam_l = lru_lam[l].reshape(2, 1, BRANCH_W)
        w_gate_l = w_gate[l].reshape(D, N_BRANCH * D).astype(BF16)
        w_branch_l = w_branch[l].astype(BF16)
        w_out_l = w_out[l].astype(BF16)
        w_fg = w_ffn_gate[l].astype(BF16)
        w_fu = w_ffn_up[l].astype(BF16)
        w_fd = w_ffn_down[l].astype(BF16)
        lru_args = (conv_w[l], cb_l, w_rg_l, b_rg_l, w_ig_l, b_ig_l, lam_l)

        px, hx = _inproj(xs, mods, l, x_row(tm_in), gm, w_in_l, tm_in)
        pc, hc = _inproj(cs, mods, l, c_row(tm_in), gm, w_in_l, tm_in)

        yc_lru, hc_state = _lru(pc, C, *lru_args, jnp.zeros((B, 2, BRANCH_W), F32))

        y_pool = _pool(px, L, w_pool_l, ps_l)
        y_sgu = _sgu(px, L, lng_l, w_sp_l, b_sp_l)
        y_lru, _ = _lru(px, L, *lru_args, hc_state)
        y_na = _nattn(px, pc, _bias_pairs(rpb[l]))
        merged = _merge(hx, (y_pool, y_sgu, y_lru, y_na), w_gate_l, w_branch_l, tm_merge)
        xm, hx2 = _outproj(merged, xs, w_out_l, mods, l, x_row(tm_out), gf, tm_out)

        if not last:
            yc_pool = _pool(pc, C, w_pool_l, ps_l)
            yc_sgu = _sgu(pc, C, lng_l, w_sp_l, b_sp_l)
            yc_na = _cattn(pc)
            merged_c = _merge(hc, (yc_pool, yc_sgu, yc_lru, yc_na), w_gate_l, w_branch_l, tm_merge)
            cm, hc2 = _outproj(merged_c, cs, w_out_l, mods, l, c_row(tm_out), gf, tm_out)
            cs = _ffn(hc2, cm, w_fg, w_fu, w_fd, mods, l, c_row(tm_ffn), gfin, False, tm_ffn)

        xs = _ffn(hx2, xm, w_fg, w_fu, w_fd, mods, l, x_row(tm_ffn), gfin, last, tm_ffn)

    return xs.reshape(B, L, D)
```

```python
import functools

import jax
import jax.numpy as jnp
from jax import lax
from jax.experimental import pallas as pl
from jax.experimental.pallas import tpu as pltpu

D_MODEL = 2048
BATCH = 8
SEQ = 2048
DEPTH = 2
CTX_LEN = 256
GRID_W = 64
N_BRANCH = 4
BRANCH_W = D_MODEL // N_BRANCH
GROUP_W = 128
N_GROUPS = BRANCH_W // GROUP_W
N_IN_SLICES = 8
IN_W = N_IN_SLICES * BRANCH_W
POOL_WINDOWS = (2, 4, 8, 16)
CHUNK = 128
CONV_W = 4
LRU_C = 8.0
NA_WIN_R = 8
NA_WIN_C = 16
D_FF = 5632
EPS = 1e-6
NEG_INF = -1e30

N_MOD = 6
COND_ROWS = 16
CTX_COND_ROW = BATCH
SUBLANES = 8
POOL_HALO = 16
CONV_HALO = 8
VMEM_LIMIT = 60 * 1024 * 1024

F32 = jnp.float32
BF16 = jnp.bfloat16

S_POOL, S_U, S_V, S_LRU, S_LG, S_Q, S_K, S_VAL = range(N_IN_SLICES)
N_F32_SLICES = S_Q
HEADS_PER_SLICE = BRANCH_W // GROUP_W
Q_COL, K_COL, V_COL = (HEADS_PER_SLICE * (s - N_F32_SLICES) for s in (S_Q, S_K, S_VAL))
NA_GROUP = 4
NA_KEY_ROWS = 12
NA_D = 2 * NA_WIN_R
FFN_SUB = 256


def _params(*sem):
    return pltpu.CompilerParams(dimension_semantics=sem, vmem_limit_bytes=VMEM_LIMIT)


def _rms_mod_rows(src_ref, dst_ref, g_ref, sc_ref, sh_ref, rows, chunk=16):
    def body(c, carry):
        r = pl.multiple_of(c * chunk, chunk)
        x = src_ref[pl.ds(r, chunk), :]
        ms = jnp.mean(x * x, axis=-1, keepdims=True)
        y = (x * lax.rsqrt(ms + EPS)) * g_ref[...]
        dst_ref[pl.ds(r, chunk), :] = (y * (1.0 + sc_ref[...]) + sh_ref[...]).astype(dst_ref.dtype)
        return carry
    lax.fori_loop(0, rows // chunk, body, 0, unroll=4)


def _mod_body(c_ref, w_ref, b_ref, o_ref):
    a = jax.nn.silu(c_ref[...]).astype(BF16)
    o_ref[...] = jnp.dot(a, w_ref[...].astype(BF16), preferred_element_type=F32) + b_ref[...]


def _adaln_all(cond, w_mod, b_mod):
    tn = 1024
    n = N_MOD * D_MODEL
    out = pl.pallas_call(
        _mod_body,
        grid=(DEPTH, n // tn),
        in_specs=[
            pl.BlockSpec((COND_ROWS, D_MODEL), lambda l, j: (0, 0)),
            pl.BlockSpec((None, D_MODEL, tn), lambda l, j: (l, 0, j)),
            pl.BlockSpec((None, 1, tn), lambda l, j: (l, 0, j)),
        ],
        out_specs=pl.BlockSpec((None, COND_ROWS, tn), lambda l, j: (l, 0, j)),
        out_shape=jax.ShapeDtypeStruct((DEPTH, COND_ROWS, n), F32),
        compiler_params=_params("parallel", "parallel"),
        name="adaln",
    )(cond, w_mod, b_mod.reshape(DEPTH, 1, n))
    return out.reshape(DEPTH * COND_ROWS * N_MOD, 1, D_MODEL)


def _mod_spec(layer, which, row_of_tile):
    def index(i, *_):
        return ((layer * COND_ROWS + row_of_tile(i)) * N_MOD + which, 0, 0)
    return pl.BlockSpec((None, 1, D_MODEL), index)


def _vec_spec(n):
    return pl.BlockSpec((1, n), lambda *_: (0, 0))


def _inproj_body(x_ref, sh_ref, sc_ref, g_ref, w_ref, p_ref, qkv_ref, h_ref, *, tm):
    j = pl.program_id(1)

    @pl.when(j == 0)
    def _():
        _rms_mod_rows(x_ref, h_ref, g_ref, sc_ref, sh_ref, tm)

    @pl.when(j < N_F32_SLICES)
    def _():
        p_ref[...] = jnp.dot(h_ref[...], w_ref[...], preferred_element_type=F32)

    @pl.when(j >= N_F32_SLICES)
    def _():
        qkv_ref[...] = jnp.dot(h_ref[...], w_ref[...], preferred_element_type=F32).astype(BF16)


def _inproj(x, mods, layer, row_of_tile, g, w, tm):
    rows = x.shape[0]
    tn = BRANCH_W
    last_f32 = N_F32_SLICES - 1
    return pl.pallas_call(
        functools.partial(_inproj_body, tm=tm),
        grid=(rows // tm, N_IN_SLICES),
        in_specs=[
            pl.BlockSpec((tm, D_MODEL), lambda i, j: (i, 0)),
            _mod_spec(layer, 0, row_of_tile),
            _mod_spec(layer, 1, row_of_tile),
            _vec_spec(D_MODEL),
            pl.BlockSpec((D_MODEL, tn), lambda i, j: (0, j)),
        ],
        out_specs=[
            pl.BlockSpec((tm, tn), lambda i, j: (i, jnp.minimum(j, last_f32))),
            pl.BlockSpec((tm, tn), lambda i, j: (i, jnp.maximum(j - N_F32_SLICES, 0))),
            pl.BlockSpec((tm, D_MODEL), lambda i, j: (i, 0)),
        ],
        out_shape=[
            jax.ShapeDtypeStruct((rows, N_F32_SLICES * BRANCH_W), F32),
            jax.ShapeDtypeStruct((rows, (N_IN_SLICES - N_F32_SLICES) * BRANCH_W), BF16),
            jax.ShapeDtypeStruct((rows, D_MODEL), BF16),
        ],
        compiler_params=_params("parallel", "arbitrary"),
        name="inproj",
    )(x, mods, mods, g, w)


def _pool_body(z_ref, w_ref, s_ref, o_ref, zp_ref, *, L):
    halo = jnp.zeros((POOL_HALO, BRANCH_W), F32)
    zp_ref[0:POOL_HALO, :] = halo
    zp_ref[POOL_HALO + L:2 * POOL_HALO + L, :] = halo
    zp_ref[POOL_HALO:POOL_HALO + L, :] = z_ref[...]
    rc = CHUNK
    for c in range(L // rc):
        r0 = c * rc
        t = r0 + lax.broadcasted_iota(jnp.int32, (rc, GROUP_W), 0)
        for gi, win in enumerate(POOL_WINDOWS):
            half = win // 2
            sl = slice(gi * GROUP_W, (gi + 1) * GROUP_W)
            acc = zp_ref[POOL_HALO + r0 - half:POOL_HALO + r0 - half + rc, sl]
            for o in range(-half + 1, half):
                acc = acc + zp_ref[POOL_HALO + r0 + o:POOL_HALO + r0 + o + rc, sl]
            cnt = (jnp.minimum(t + half, L) - jnp.maximum(t - half, 0)).astype(F32)
            pooled = acc / cnt - z_ref[r0:r0 + rc, sl]
            y = jnp.dot(pooled.astype(BF16), w_ref[gi], preferred_element_type=F32)
            o_ref[r0:r0 + rc, sl] = (y * s_ref[:, sl]).astype(BF16)


def _pool(p, L, w_pool, pool_scale):
    nb = p.shape[0] // L
    return pl.pallas_call(
        functools.partial(_pool_body, L=L),
        grid=(nb,),
        in_specs=[
            pl.BlockSpec((L, BRANCH_W), lambda b: (b, S_POOL)),
            pl.BlockSpec((N_GROUPS, GROUP_W, GROUP_W), lambda b: (0, 0, 0)),
            _vec_spec(BRANCH_W),
        ],
        out_specs=pl.BlockSpec((L, BRANCH_W), lambda b: (b, 0)),
        out_shape=jax.ShapeDtypeStruct((p.shape[0], BRANCH_W), BF16),
        scratch_shapes=[pltpu.VMEM((L + 2 * POOL_HALO, BRANCH_W), F32)],
        compiler_params=_params("parallel"),
        name="pool",
    )(p, w_pool, pool_scale)


def _sgu_body(u_ref, v_ref, g_ref, w_ref, b_ref, o_ref, *, L):
    for n in range(L // CHUNK):
        r0 = n * CHUNK
        v = v_ref[r0:r0 + CHUNK, :]
        vc = v - jnp.mean(v, axis=-1, keepdims=True)
        var = jnp.mean(vc * vc, axis=-1, keepdims=True)
        vn = ((vc * lax.rsqrt(var + EPS)) * g_ref[...]).astype(BF16)
        for gi in range(N_GROUPS):
            sl = slice(gi * GROUP_W, (gi + 1) * GROUP_W)
            mixed = jnp.dot(w_ref[gi], vn[:, sl], preferred_element_type=F32) + b_ref[gi]
            o_ref[r0:r0 + CHUNK, sl] = (u_ref[r0:r0 + CHUNK, sl] * mixed).astype(BF16)


def _sgu(p, L, ln_g, w_sp, b_sp):
    nb = p.shape[0] // L
    return pl.pallas_call(
        functools.partial(_sgu_body, L=L),
        grid=(nb,),
        in_specs=[
            pl.BlockSpec((L, BRANCH_W), lambda b: (b, S_U)),
            pl.BlockSpec((L, BRANCH_W), lambda b: (b, S_V)),
            _vec_spec(BRANCH_W),
            pl.BlockSpec((N_GROUPS, CHUNK, CHUNK), lambda b: (0, 0, 0)),
            pl.BlockSpec((N_GROUPS, CHUNK, 1), lambda b: (0, 0, 0)),
        ],
        out_specs=pl.BlockSpec((L, BRANCH_W), lambda b: (b, 0)),
        out_shape=jax.ShapeDtypeStruct((p.shape[0], BRANCH_W), BF16),
        compiler_params=_params("parallel"),
        name="sgu",
    )(p, p, ln_g, w_sp, b_sp)


def _lru_scan(a_ref, b_ref, h0, nblk, reverse, emit):
    row = lax.broadcasted_iota(jnp.int32, (SUBLANES, BRANCH_W), 0)

    def body(jj, h):
        j = (nblk - 1 - jj) if reverse else jj
        r = pl.multiple_of(j * SUBLANES, SUBLANES)
        a = a_ref[pl.ds(r, SUBLANES), :]
        b = b_ref[pl.ds(r, SUBLANES), :]
        for s in (1, 2, 4):
            shift = (SUBLANES - s) if reverse else s
            a_s = pltpu.roll(a, shift, 0)
            b_s = pltpu.roll(b, shift, 0)
            m = (row < SUBLANES - s) if reverse else (row >= s)
            b = jnp.where(m, a * b_s + b, b)
            a = jnp.where(m, a * a_s, a)
        hh = a * h + b
        emit(r, hh)
        edge = hh[0:1, :] if reverse else hh[SUBLANES - 1:SUBLANES, :]
        return jnp.broadcast_to(edge, (SUBLANES, BRANCH_W))

    return lax.fori_loop(0, nblk, body, h0, unroll=4)


def _lru_body(z_ref, lg_ref, cw_ref, cb_ref, wr_ref, br_ref, wi_ref, bi_ref, lam_ref, h0_ref,
              y_ref, hl_ref, zp_ref, a_ref, b_ref, hs_ref, *, L):
    halo = jnp.zeros((CONV_HALO, BRANCH_W), F32)
    zp_ref[0:CONV_HALO, :] = halo
    zp_ref[CONV_HALO + L:2 * CONV_HALO + L, :] = halo
    zp_ref[CONV_HALO:CONV_HALO + L, :] = z_ref[...]
    left = CONV_W // 2
    rc = CHUNK
    nblk = L // SUBLANES

    def conv_chunk(r0):
        out = zp_ref[CONV_HALO + r0 - left:CONV_HALO + r0 - left + rc, :] * cw_ref[0:1, :] + cb_ref[...]
        for k in range(1, CONV_W):
            lo = CONV_HALO + r0 - left + k
            out = out + zp_ref[lo:lo + rc, :] * cw_ref[k:k + 1, :]
        return out

    def sigmoid(t):
        return 0.5 * jnp.tanh(0.5 * t) + 0.5

    sp = [jax.nn.softplus(-lam_ref[d]) for d in range(2)]
    for c in range(L // rc):
        r0 = c * rc
        cv = conv_chunk(r0)
        cvb = cv.astype(BF16)
        for d in range(2):
            for gi in range(N_GROUPS):
                sl = slice(gi * GROUP_W, (gi + 1) * GROUP_W)
                rg = sigmoid(jnp.dot(cvb[:, sl], wr_ref[d, gi], preferred_element_type=F32) + br_ref[d][:, sl])
                ig = sigmoid(jnp.dot(cvb[:, sl], wi_ref[d, gi], preferred_element_type=F32) + bi_ref[d][:, sl])
                log_a = (-LRU_C * rg) * sp[d][:, sl]
                a_ref[d, r0:r0 + rc, sl] = jnp.exp(log_a)
                th = jnp.tanh(log_a)
                b_ref[d, r0:r0 + rc, sl] = jnp.sqrt(-2.0 * th / (1.0 - th)) * (ig * cv[:, sl])

    for d in range(2):
        reverse = d == 1
        h0 = jnp.broadcast_to(h0_ref[0, d:d + 1, :], (SUBLANES, BRANCH_W))
        if not reverse:
            def emit(r, hh):
                hs_ref[pl.ds(r, SUBLANES), :] = hh
        else:
            def emit(r, hh):
                tot = hs_ref[pl.ds(r, SUBLANES), :] + hh
                y_ref[pl.ds(r, SUBLANES), :] = (tot * jax.nn.gelu(lg_ref[pl.ds(r, SUBLANES), :])).astype(BF16)
        h_end = _lru_scan(a_ref.at[d], b_ref.at[d], h0, nblk, reverse, emit)
        hl_ref[0, d:d + 1, :] = h_end[0:1, :]


def _lru(p, L, conv_w, conv_b, w_rg, b_rg, w_ig, b_ig, lam, h0):
    nb = p.shape[0] // L
    full = lambda *shape: pl.BlockSpec(shape, lambda b: (0,) * len(shape))
    return pl.pallas_call(
        functools.partial(_lru_body, L=L),
        grid=(nb,),
        in_specs=[
            pl.BlockSpec((L, BRANCH_W), lambda b: (b, S_LRU)),
            pl.BlockSpec((L, BRANCH_W), lambda b: (b, S_LG)),
            full(CONV_W, BRANCH_W),
            full(1, BRANCH_W),
            full(2, N_GROUPS, GROUP_W, GROUP_W),
            full(2, 1, BRANCH_W),
            full(2, N_GROUPS, GROUP_W, GROUP_W),
            full(2, 1, BRANCH_W),
            full(2, 1, BRANCH_W),
            pl.BlockSpec((1, 2, BRANCH_W), lambda b: (b, 0, 0)),
        ],
        out_specs=[
            pl.BlockSpec((L, BRANCH_W), lambda b: (b, 0)),
            pl.BlockSpec((1, 2, BRANCH_W), lambda b: (b, 0, 0)),
        ],
        out_shape=[
            jax.ShapeDtypeStruct((p.shape[0], BRANCH_W), BF16),
            jax.ShapeDtypeStruct((nb, 2, BRANCH_W), F32),
        ],
        scratch_shapes=[
            pltpu.VMEM((L + 2 * CONV_HALO, BRANCH_W), F32),
            pltpu.VMEM((2, L, BRANCH_W), F32),
            pltpu.VMEM((2, L, BRANCH_W), F32),
            pltpu.VMEM((L, BRANCH_W), F32),
        ],
        compiler_params=_params("parallel"),
        name="lru",
    )(p, p, conv_w, conv_b, w_rg, b_rg, w_ig, b_ig, lam, h0)


_NT = (((1,), (1,)), ((), ()))


def _softmax_pv(scores, values):
    m = scores[0].max(axis=-1, keepdims=True)
    for s in scores[1:]:
        m = jnp.maximum(m, s.max(axis=-1, keepdims=True))
    es = [jnp.exp(s - m) for s in scores]
    tot = es[0].sum(axis=-1, keepdims=True)
    for e in es[1:]:
        tot = tot + e.sum(axis=-1, keepdims=True)
    inv = 1.0 / tot
    out = None
    for e, v in zip(es, values):
        o = jnp.dot((e * inv).astype(BF16), v, preferred_element_type=F32)
        out = o if out is None else out + o
    return out


def _nattn_body(q_ref, k_ref, v_ref, kc_ref, vc_ref, t_ref, o_ref):
    rows = SEQ // GRID_W
    nq = NA_GROUP * GRID_W
    nk = NA_KEY_ROWS * GRID_W
    scale = GROUP_W ** -0.5
    kc = kc_ref[...]
    vc = vc_ref[...]

    def in_window(kr, row0):
        return ((kr >= row0) & (kr < row0 + NA_WIN_R)).astype(jnp.int32)

    def body(gi, carry):
        r0 = gi * NA_GROUP
        base = jnp.clip(r0 - NA_WIN_R // 2, 0, rows - NA_KEY_ROWS)
        qs = pl.multiple_of(r0 * GRID_W, nq)
        ks = pl.multiple_of(base * GRID_W, GRID_W)
        q = q_ref[pl.ds(qs, nq), :]
        kw = k_ref[pl.ds(ks, nk), :]
        vw = v_ref[pl.ds(ks, nk), :]
        s = lax.dot_general(q, kw, _NT, preferred_element_type=F32) * scale
        bias_rows = []
        for g in range(NA_GROUP):
            r = r0 + g
            row0 = jnp.clip(r - NA_WIN_R // 2, 0, rows - NA_WIN_R)
            pieces = []
            for jj in range(NA_KEY_ROWS // 2):
                kr = base + 2 * jj
                variant = in_window(kr, row0) + 2 * in_window(kr + 1, row0)
                e = jnp.clip(kr - r + NA_WIN_R, 0, NA_D - 1)
                pieces.append(t_ref[variant * NA_D + e])
            bias_rows.append(jnp.concatenate(pieces, axis=1))
        bias = jnp.concatenate(bias_rows, axis=0)
        s = jnp.where(bias > 0.5 * NEG_INF, s + bias, NEG_INF)
        sc = lax.dot_general(q, kc, _NT, preferred_element_type=F32) * scale
        o_ref[pl.ds(qs, nq), :] = _softmax_pv([s, sc], [vw, vc]).astype(BF16)
        return carry

    lax.fori_loop(0, rows // NA_GROUP, body, 0)


def _nattn(qkv, qkv_c, bias_table):
    return pl.pallas_call(
        _nattn_body,
        grid=(BATCH, N_GROUPS),
        in_specs=[
            pl.BlockSpec((SEQ, GROUP_W), lambda b, h: (b, Q_COL + h)),
            pl.BlockSpec((SEQ, GROUP_W), lambda b, h: (b, K_COL + h)),
            pl.BlockSpec((SEQ, GROUP_W), lambda b, h: (b, V_COL + h)),
            pl.BlockSpec((CTX_LEN, GROUP_W), lambda b, h: (b, K_COL + h)),
            pl.BlockSpec((CTX_LEN, GROUP_W), lambda b, h: (b, V_COL + h)),
            pl.BlockSpec((None, 4 * NA_D, GRID_W, 2 * GRID_W), lambda b, h: (h, 0, 0, 0)),
        ],
        out_specs=pl.BlockSpec((SEQ, GROUP_W), lambda b, h: (b, h)),
        out_shape=jax.ShapeDtypeStruct((qkv.shape[0], BRANCH_W), BF16),
        compiler_params=_params("parallel", "parallel"),
        name="nattn",
    )(qkv, qkv, qkv, qkv_c, qkv_c, bias_table)


def _cattn_body(q_ref, k_ref, v_ref, o_ref):
    scale = GROUP_W ** -0.5
    s = lax.dot_general(q_ref[...], k_ref[...], _NT, preferred_element_type=F32) * scale
    o_ref[...] = _softmax_pv([s], [v_ref[...]]).astype(BF16)


def _cattn(qkv_c):
    return pl.pallas_call(
        _cattn_body,
        grid=(BATCH, N_GROUPS),
        in_specs=[
            pl.BlockSpec((CTX_LEN, GROUP_W), lambda b, h: (b, Q_COL + h)),
            pl.BlockSpec((CTX_LEN, GROUP_W), lambda b, h: (b, K_COL + h)),
            pl.BlockSpec((CTX_LEN, GROUP_W), lambda b, h: (b, V_COL + h)),
        ],
        out_specs=pl.BlockSpec((CTX_LEN, GROUP_W), lambda b, h: (b, h)),
        out_shape=jax.ShapeDtypeStruct((qkv_c.shape[0], BRANCH_W), BF16),
        compiler_params=_params("parallel", "parallel"),
        name="cattn",
    )(qkv_c, qkv_c, qkv_c)


def _bias_table(rpb):
    col = jnp.arange(GRID_W)
    col_start = jnp.clip(col - NA_WIN_C // 2, 0, GRID_W - NA_WIN_C)
    col_ok = (col[None, :] >= col_start[:, None]) & (col[None, :] < col_start[:, None] + NA_WIN_C)
    d_col = jnp.clip(col[None, :] - col[:, None] + (NA_WIN_C - 1), 0, 2 * NA_WIN_C - 2)
    tab = jnp.where(col_ok[None, None], rpb[:, :, d_col].astype(F32), NEG_INF)
    off = jnp.full_like(tab[:, :1], NEG_INF)
    lo = jnp.concatenate([off, tab], axis=1)
    hi = jnp.concatenate([tab, off], axis=1)
    none = jnp.full_like(lo, NEG_INF)
    variants = [jnp.concatenate([lo if v & 1 else none, hi if v & 2 else none], axis=-1) for v in range(4)]
    return jnp.concatenate(variants, axis=1)


def _merge_body(h_ref, y0_ref, y1_ref, y2_ref, y3_ref, g0_ref, g1_ref, g2_ref, g3_ref, wb_ref, o_ref):
    h = h_ref[...]
    acc = None
    for n, (y_ref, g_ref) in enumerate(((y0_ref, g0_ref), (y1_ref, g1_ref), (y2_ref, g2_ref), (y3_ref, g3_ref))):
        gate = jax.nn.sigmoid(jnp.dot(h, g_ref[...], preferred_element_type=F32))
        term = gate * jnp.dot(y_ref[...], wb_ref[n], preferred_element_type=F32)
        acc = term if acc is None else acc + term
    o_ref[...] = acc.astype(BF16)


def _merge(h, ys, w_gate, w_branch, tm):
    rows = h.shape[0]
    tn = 256
    nj = D_MODEL // tn
    gate_spec = lambda n: pl.BlockSpec((D_MODEL, tn), lambda i, j: (0, n * nj + j))
    y_spec = pl.BlockSpec((tm, BRANCH_W), lambda i, j: (i, 0))
    return pl.pallas_call(
        _merge_body,
        grid=(rows // tm, nj),
        in_specs=[pl.BlockSpec((tm, D_MODEL), lambda i, j: (i, 0)), y_spec, y_spec, y_spec, y_spec,
                  gate_spec(0), gate_spec(1), gate_spec(2), gate_spec(3),
                  pl.BlockSpec((N_BRANCH, BRANCH_W, tn), lambda i, j: (0, 0, j))],
        out_specs=pl.BlockSpec((tm, tn), lambda i, j: (i, j)),
        out_shape=jax.ShapeDtypeStruct((rows, D_MODEL), BF16),
        compiler_params=_params("parallel", "arbitrary"),
        name="merge",
    )(h, *ys, w_gate, w_gate, w_gate, w_gate, w_branch)


def _outproj_body(m_ref, x_ref, w_ref, gt_ref, g_ref, sh_ref, sc_ref, xm_ref, h2_ref, *, tm):
    xm_ref[...] = x_ref[...] + gt_ref[...] * jnp.dot(m_ref[...], w_ref[...], preferred_element_type=F32)
    _rms_mod_rows(xm_ref, h2_ref, g_ref, sc_ref, sh_ref, tm)


def _outproj(merged, x, w_out, mods, layer, row_of_tile, g_ffn, tm):
    rows = x.shape[0]
    row_spec = pl.BlockSpec((tm, D_MODEL), lambda i: (i, 0))
    return pl.pallas_call(
        functools.partial(_outproj_body, tm=tm),
        grid=(rows // tm,),
        in_specs=[row_spec, row_spec,
                  pl.BlockSpec((D_MODEL, D_MODEL), lambda i: (0, 0)),
                  _mod_spec(layer, 2, row_of_tile),
                  _vec_spec(D_MODEL),
                  _mod_spec(layer, 3, row_of_tile),
                  _mod_spec(layer, 4, row_of_tile)],
        out_specs=[row_spec, row_spec],
        out_shape=[jax.ShapeDtypeStruct((rows, D_MODEL), F32),
                   jax.ShapeDtypeStruct((rows, D_MODEL), BF16)],
        compiler_params=_params("parallel"),
        name="outproj",
    )(merged, x, w_out, mods, g_ffn, mods, mods)


def _ffn_body(h_ref, wg_ref, wu_ref, wd_ref, xm_ref, gt_ref, gf_ref, o_ref, a_ref, *, tm, tf, final):
    f = pl.program_id(1)

    h = h_ref[...]
    for s in range(tf // FFN_SUB):
        sl = slice(s * FFN_SUB, (s + 1) * FFN_SUB)
        g = jnp.dot(h, wg_ref[:, sl], preferred_element_type=F32)
        u = jnp.dot(h, wu_ref[:, sl], preferred_element_type=F32)
        a_ref[:, sl] = (jax.nn.silu(g) * u).astype(BF16)

    @pl.when(f == 0)
    def _():
        o_ref[...] = jnp.dot(a_ref[...], wd_ref[...], preferred_element_type=F32)

    @pl.when(f > 0)
    def _():
        o_ref[...] += jnp.dot(a_ref[...], wd_ref[...], preferred_element_type=F32)

    @pl.when(f == pl.num_programs(1) - 1)
    def _():
        chunk = 16

        def body(c, carry):
            r = pl.multiple_of(c * chunk, chunk)
            y = xm_ref[pl.ds(r, chunk), :] + gt_ref[...] * o_ref[pl.ds(r, chunk), :]
            if final:
                ms = jnp.mean(y * y, axis=-1, keepdims=True)
                y = (y * lax.rsqrt(ms + EPS)) * gf_ref[...]
            o_ref[pl.ds(r, chunk), :] = y
            return carry
        lax.fori_loop(0, tm // chunk, body, 0, unroll=4)


def _ffn(h2, xm, w_g, w_u, w_d, mods, layer, row_of_tile, g_final, final, tm):
    rows = xm.shape[0]
    tf = 512
    row_spec = pl.BlockSpec((tm, D_MODEL), lambda i, f: (i, 0))
    return pl.pallas_call(
        functools.partial(_ffn_body, tm=tm, tf=tf, final=final),
        grid=(rows // tm, D_FF // tf),
        in_specs=[row_spec,
                  pl.BlockSpec((D_MODEL, tf), lambda i, f: (0, f)),
                  pl.BlockSpec((D_MODEL, tf), lambda i, f: (0, f)),
                  pl.BlockSpec((tf, D_MODEL), lambda i, f: (f, 0)),
                  row_spec,
                  _mod_spec(layer, 5, row_of_tile),
                  _vec_spec(D_MODEL)],
        out_specs=row_spec,
        out_shape=jax.ShapeDtypeStruct((rows, D_MODEL), F32),
        scratch_shapes=[pltpu.VMEM((tm, tf), BF16)],
        compiler_params=_params("parallel", "arbitrary"),
        name="ffn",
    )(h2, w_g, w_u, w_d, xm, mods, g_final)


def kernel(x, c, ctx, c_ctx, w_mod, b_mod, g_mix, g_ffn, g_final, w_in, w_pool, pool_scale, gmlp_ln_g, w_sp, b_sp, conv_w, conv_b, w_rg, b_rg, w_ig, b_ig, lru_lam, rpb, w_branch, w_gate, w_out, w_ffn_gate, w_ffn_up, w_ffn_down):
    B, L, D = x.shape
    C = ctx.shape[1]
    assert (B, L, D, C) == (BATCH, SEQ, D_MODEL, CTX_LEN)
    xs = x.reshape(B * L, D)
    cs = ctx.reshape(B * C, D)

    cond = jnp.concatenate([c, c_ctx[None, :], jnp.zeros((COND_ROWS - B - 1, D), F32)], axis=0)
    mods = _adaln_all(cond, w_mod, b_mod)

    tm_in, tm_merge, tm_out, tm_ffn = 1024, 1024, 512, 1024
    x_row = lambda tm: (lambda i: i // (SEQ // tm))
    c_row = lambda tm: (lambda i: CTX_COND_ROW)
    gfin = g_final.reshape(1, D)

    for l in range(DEPTH):
        last = l == DEPTH - 1
        gm = g_mix[l].reshape(1, D)
        gf = g_ffn[l].reshape(1, D)
        w_in_l = w_in[l].astype(BF16)
        w_pool_l = w_pool[l].astype(BF16)
        ps_l = pool_scale[l].reshape(1, BRANCH_W)
        lng_l = gmlp_ln_g[l].reshape(1, BRANCH_W)
        w_sp_l = w_sp[l].astype(BF16)
        b_sp_l = b_sp[l].reshape(N_GROUPS, CHUNK, 1)
        cb_l = conv_b[l].reshape(1, BRANCH_W)
        w_rg_l = w_rg[l].astype(BF16)
        w_ig_l = w_ig[l].astype(BF16)
        b_rg_l = b_rg[l].reshape(2, 1, BRANCH_W)
        b_ig_l = b_ig[l].reshape(2, 1, BRANCH_W)
        lam_l = lru_lam[l].reshape(2, 1, BRANCH_W)
        w_gate_l = w_gate[l].reshape(D, N_BRANCH * D).astype(BF16)
        w_branch_l = w_branch[l].astype(BF16)
        w_out_l = w_out[l].astype(BF16)
        w_fg = w_ffn_gate[l].astype(BF16)
        w_fu = w_ffn_up[l].astype(BF16)
        w_fd = w_ffn_down[l].astype(BF16)
        lru_args = (conv_w[l], cb_l, w_rg_l, b_rg_l, w_ig_l, b_ig_l, lam_l)

        px, qkv, hx = _inproj(xs, mods, l, x_row(tm_in), gm, w_in_l, tm_in)
        pc, qkv_c, hc = _inproj(cs, mods, l, c_row(tm_in), gm, w_in_l, tm_in)

        yc_lru, hc_state = _lru(pc, C, *lru_args, jnp.zeros((B, 2, BRANCH_W), F32))

        y_pool = _pool(px, L, w_pool_l, ps_l)
        y_sgu = _sgu(px, L, lng_l, w_sp_l, b_sp_l)
        y_lru, _ = _lru(px, L, *lru_args, hc_state)
        y_na = _nattn(qkv, qkv_c, _bias_table(rpb[l]))
        merged = _merge(hx, (y_pool, y_sgu, y_lru, y_na), w_gate_l, w_branch_l, tm_merge)
        xm, hx2 = _outproj(merged, xs, w_out_l, mods, l, x_row(tm_out), gf, tm_out)

        if not last:
            yc_pool = _pool(pc, C, w_pool_l, ps_l)
            yc_sgu = _sgu(pc, C, lng_l, w_sp_l, b_sp_l)
            yc_na = _cattn(qkv_c)
            merged_c = _merge(hc, (yc_pool, yc_sgu, yc_lru, yc_na), w_gate_l, w_branch_l, tm_merge)
            cm, hc2 = _outproj(merged_c, cs, w_out_l, mods, l, c_row(tm_out), gf, tm_out)
            cs = _ffn(hc2, cm, w_fg, w_fu, w_fd, mods, l, c_row(tm_ffn), gfin, False, tm_ffn)

        xs = _ffn(hx2, xm, w_fg, w_fu, w_fd, mods, l, x_row(tm_ffn), gfin, last, tm_ffn)

    return xs.reshape(B, L, D)
```

```python
import functools

import jax
import jax.numpy as jnp
from jax import lax
from jax.experimental import pallas as pl
from jax.experimental.pallas import tpu as pltpu

D_MODEL = 2048
BATCH = 8
SEQ = 2048
DEPTH = 2
CTX_LEN = 256
GRID_W = 64
N_BRANCH = 4
BRANCH_W = D_MODEL // N_BRANCH
GROUP_W = 128
N_GROUPS = BRANCH_W // GROUP_W
N_IN_SLICES = 8
IN_W = N_IN_SLICES * BRANCH_W
POOL_WINDOWS = (2, 4, 8, 16)
CHUNK = 128
CONV_W = 4
LRU_C = 8.0
NA_WIN_R = 8
NA_WIN_C = 16
D_FF = 5632
EPS = 1e-6
NEG_INF = -1e30

N_MOD = 6
COND_ROWS = 16
CTX_COND_ROW = BATCH
SUBLANES = 8
LANES = 128
POOL_HALO = 16
CONV_HALO = 8
VMEM_LIMIT = 60 * 1024 * 1024

F32 = jnp.float32
BF16 = jnp.bfloat16

S_POOL, S_U, S_V, S_LRU, S_LG, S_Q, S_K, S_VAL = range(N_IN_SLICES)
N_F32_SLICES = S_Q
HEADS_PER_SLICE = BRANCH_W // GROUP_W
Q_COL, K_COL, V_COL = (HEADS_PER_SLICE * (s - N_F32_SLICES) for s in (S_Q, S_K, S_VAL))
NA_GROUP = 4
NA_KEY_ROWS = 12
NA_D = 2 * NA_WIN_R
FFN_SUB = 256
NORM_ROWS = 16

TM_IN, TM_MERGE, TM_OUT, TM_FFN = 1024, 1024, 512, 1024


def _params(*sem):
    return pltpu.CompilerParams(dimension_semantics=sem, vmem_limit_bytes=VMEM_LIMIT)


def _layer_spec(layer, *shape):
    zeros = (0,) * len(shape)
    return pl.BlockSpec((None,) + shape, lambda *_: (layer,) + zeros)


def _mod_spec(layer, which, row_of_tile):
    def index(i, *_):
        return ((layer * COND_ROWS + row_of_tile(i)) * N_MOD + which, 0, 0)
    return pl.BlockSpec((None, 1, D_MODEL), index)


def _latent_row(tm):
    return lambda i: i // (SEQ // tm)


def _context_row(tm):
    return lambda i: CTX_COND_ROW


def _rms_mod_rows(src_ref, dst_ref, g_ref, sc_ref, sh_ref, rows):
    def body(c, carry):
        r = pl.multiple_of(c * NORM_ROWS, NORM_ROWS)
        x = src_ref[pl.ds(r, NORM_ROWS), :]
        ms = jnp.mean(x * x, axis=-1, keepdims=True)
        y = (x * lax.rsqrt(ms + EPS)) * g_ref[...]
        dst_ref[pl.ds(r, NORM_ROWS), :] = (y * (1.0 + sc_ref[...]) + sh_ref[...]).astype(dst_ref.dtype)
        return carry
    lax.fori_loop(0, rows // NORM_ROWS, body, 0, unroll=4)


def _mod_body(c_ref, w_ref, b_ref, o_ref):
    @pl.when(pl.program_id(1) == 0)
    def _():
        o_ref[...] = jnp.broadcast_to(b_ref[...], o_ref.shape)
    a = jax.nn.silu(c_ref[...]).astype(BF16)
    o_ref[...] += jnp.dot(a, w_ref[...].astype(BF16), preferred_element_type=F32)


def _adaln_all(cond, w_mod, b_mod):
    tk = 256
    n = N_MOD * D_MODEL
    out = pl.pallas_call(
        _mod_body,
        grid=(DEPTH, D_MODEL // tk),
        in_specs=[
            pl.BlockSpec((COND_ROWS, tk), lambda l, k: (0, k)),
            pl.BlockSpec((None, tk, n), lambda l, k: (l, k, 0)),
            pl.BlockSpec((None, 1, n), lambda l, k: (l, 0, 0)),
        ],
        out_specs=pl.BlockSpec((None, COND_ROWS, n), lambda l, k: (l, 0, 0)),
        out_shape=jax.ShapeDtypeStruct((DEPTH, COND_ROWS, n), F32),
        compiler_params=_params("parallel", "arbitrary"),
        name="adaln",
    )(cond, w_mod, b_mod.reshape(DEPTH, 1, n))
    return out.reshape(DEPTH * COND_ROWS * N_MOD, 1, D_MODEL)


def _inproj_body(x_ref, sh_ref, sc_ref, g_ref, w_ref, p_ref, qkv_ref, h_ref, *, tm):
    j = pl.program_id(1)

    @pl.when(j == 0)
    def _():
        _rms_mod_rows(x_ref, h_ref, g_ref, sc_ref, sh_ref, tm)

    @pl.when(j < N_F32_SLICES)
    def _():
        p_ref[...] = jnp.dot(h_ref[...], w_ref[...], preferred_element_type=F32)

    @pl.when(j >= N_F32_SLICES)
    def _():
        qkv_ref[...] = jnp.dot(h_ref[...], w_ref[...], preferred_element_type=F32).astype(BF16)


def _inproj(x, mods, layer, row_of_tile, g_mix, w_in):
    rows = x.shape[0]
    tm, tn = TM_IN, BRANCH_W
    last_f32 = N_F32_SLICES - 1
    return pl.pallas_call(
        functools.partial(_inproj_body, tm=tm),
        grid=(rows // tm, N_IN_SLICES),
        in_specs=[
            pl.BlockSpec((tm, D_MODEL), lambda i, j: (i, 0)),
            _mod_spec(layer, 0, row_of_tile(tm)),
            _mod_spec(layer, 1, row_of_tile(tm)),
            _layer_spec(layer, 1, D_MODEL),
            pl.BlockSpec((None, D_MODEL, tn), lambda i, j: (layer, 0, j)),
        ],
        out_specs=[
            pl.BlockSpec((tm, tn), lambda i, j: (i, jnp.minimum(j, last_f32))),
            pl.BlockSpec((tm, tn), lambda i, j: (i, jnp.maximum(j - N_F32_SLICES, 0))),
            pl.BlockSpec((tm, D_MODEL), lambda i, j: (i, 0)),
        ],
        out_shape=[
            jax.ShapeDtypeStruct((rows, N_F32_SLICES * BRANCH_W), F32),
            jax.ShapeDtypeStruct((rows, (N_IN_SLICES - N_F32_SLICES) * BRANCH_W), BF16),
            jax.ShapeDtypeStruct((rows, D_MODEL), BF16),
        ],
        compiler_params=_params("parallel", "arbitrary"),
        name="inproj",
    )(x, mods, mods, g_mix, w_in)


def _pool_body(z_ref, w_ref, s_ref, o_ref, zp_ref, *, L):
    halo = jnp.zeros((POOL_HALO, BRANCH_W), F32)
    zp_ref[0:POOL_HALO, :] = halo
    zp_ref[POOL_HALO + L:2 * POOL_HALO + L, :] = halo
    zp_ref[POOL_HALO:POOL_HALO + L, :] = z_ref[...]
    rc = CHUNK
    for c in range(L // rc):
        r0 = c * rc
        t = r0 + lax.broadcasted_iota(jnp.int32, (rc, GROUP_W), 0)
        for gi, win in enumerate(POOL_WINDOWS):
            half = win // 2
            sl = slice(gi * GROUP_W, (gi + 1) * GROUP_W)
            acc = zp_ref[POOL_HALO + r0 - half:POOL_HALO + r0 - half + rc, sl]
            for o in range(-half + 1, half):
                acc = acc + zp_ref[POOL_HALO + r0 + o:POOL_HALO + r0 + o + rc, sl]
            cnt = (jnp.minimum(t + half, L) - jnp.maximum(t - half, 0)).astype(F32)
            pooled = acc / cnt - z_ref[r0:r0 + rc, sl]
            y = jnp.dot(pooled.astype(BF16), w_ref[gi], preferred_element_type=F32)
            o_ref[r0:r0 + rc, sl] = (y * s_ref[:, sl]).astype(BF16)


def _pool(p, L, layer, w_pool, pool_scale):
    nb = p.shape[0] // L
    return pl.pallas_call(
        functools.partial(_pool_body, L=L),
        grid=(nb,),
        in_specs=[
            pl.BlockSpec((L, BRANCH_W), lambda b: (b, S_POOL)),
            _layer_spec(layer, N_GROUPS, GROUP_W, GROUP_W),
            _layer_spec(layer, 1, BRANCH_W),
        ],
        out_specs=pl.BlockSpec((L, BRANCH_W), lambda b: (b, 0)),
        out_shape=jax.ShapeDtypeStruct((p.shape[0], BRANCH_W), BF16),
        scratch_shapes=[pltpu.VMEM((L + 2 * POOL_HALO, BRANCH_W), F32)],
        compiler_params=_params("parallel"),
        name="pool",
    )(p, w_pool, pool_scale)


def _sgu_body(u_ref, v_ref, g_ref, w_ref, b_ref, o_ref, *, L):
    for n in range(L // CHUNK):
        r0 = n * CHUNK
        v = v_ref[r0:r0 + CHUNK, :]
        vc = v - jnp.mean(v, axis=-1, keepdims=True)
        var = jnp.mean(vc * vc, axis=-1, keepdims=True)
        vn = ((vc * lax.rsqrt(var + EPS)) * g_ref[...]).astype(BF16)
        for gi in range(N_GROUPS):
            sl = slice(gi * GROUP_W, (gi + 1) * GROUP_W)
            mixed = jnp.dot(w_ref[gi], vn[:, sl], preferred_element_type=F32) + b_ref[gi]
            o_ref[r0:r0 + CHUNK, sl] = (u_ref[r0:r0 + CHUNK, sl] * mixed).astype(BF16)


def _sgu(p, L, layer, ln_g, w_sp, b_sp):
    nb = p.shape[0] // L
    return pl.pallas_call(
        functools.partial(_sgu_body, L=L),
        grid=(nb,),
        in_specs=[
            pl.BlockSpec((L, BRANCH_W), lambda b: (b, S_U)),
            pl.BlockSpec((L, BRANCH_W), lambda b: (b, S_V)),
            _layer_spec(layer, 1, BRANCH_W),
            _layer_spec(layer, N_GROUPS, CHUNK, CHUNK),
            _layer_spec(layer, N_GROUPS, CHUNK, 1),
        ],
        out_specs=pl.BlockSpec((L, BRANCH_W), lambda b: (b, 0)),
        out_shape=jax.ShapeDtypeStruct((p.shape[0], BRANCH_W), BF16),
        compiler_params=_params("parallel"),
        name="sgu",
    )(p, p, ln_g, w_sp, b_sp)


def _lru_scan(a_ref, b_ref, h0, nblk, reverse, emit):
    row = lax.broadcasted_iota(jnp.int32, (SUBLANES, BRANCH_W), 0)

    def body(jj, h):
        j = (nblk - 1 - jj) if reverse else jj
        r = pl.multiple_of(j * SUBLANES, SUBLANES)
        a = a_ref[pl.ds(r, SUBLANES), :]
        b = b_ref[pl.ds(r, SUBLANES), :]
        for s in (1, 2, 4):
            shift = (SUBLANES - s) if reverse else s
            a_s = pltpu.roll(a, shift, 0)
            b_s = pltpu.roll(b, shift, 0)
            m = (row < SUBLANES - s) if reverse else (row >= s)
            b = jnp.where(m, a * b_s + b, b)
            a = jnp.where(m, a * a_s, a)
        hh = a * h + b
        emit(r, hh)
        edge = hh[0:1, :] if reverse else hh[SUBLANES - 1:SUBLANES, :]
        return jnp.broadcast_to(edge, (SUBLANES, BRANCH_W))

    return lax.fori_loop(0, nblk, body, h0, unroll=4)


def _lru_body(z_ref, lg_ref, cw_ref, cb_ref, wr_ref, br_ref, wi_ref, bi_ref, lam_ref, h0_ref,
              y_ref, hl_ref, zp_ref, a_ref, b_ref, hs_ref, *, L):
    halo = jnp.zeros((CONV_HALO, BRANCH_W), F32)
    zp_ref[0:CONV_HALO, :] = halo
    zp_ref[CONV_HALO + L:2 * CONV_HALO + L, :] = halo
    zp_ref[CONV_HALO:CONV_HALO + L, :] = z_ref[...]
    left = CONV_W // 2
    rc = CHUNK
    nblk = L // SUBLANES

    def conv_chunk(r0):
        out = zp_ref[CONV_HALO + r0 - left:CONV_HALO + r0 - left + rc, :] * cw_ref[0:1, :] + cb_ref[...]
        for k in range(1, CONV_W):
            lo = CONV_HALO + r0 - left + k
            out = out + zp_ref[lo:lo + rc, :] * cw_ref[k:k + 1, :]
        return out

    def sigmoid(t):
        return 0.5 * jnp.tanh(0.5 * t) + 0.5

    sp = [jax.nn.softplus(-lam_ref[d]) for d in range(2)]
    for c in range(L // rc):
        r0 = c * rc
        cv = conv_chunk(r0)
        cvb = cv.astype(BF16)
        for d in range(2):
            for gi in range(N_GROUPS):
                sl = slice(gi * GROUP_W, (gi + 1) * GROUP_W)
                rg = sigmoid(jnp.dot(cvb[:, sl], wr_ref[d, gi], preferred_element_type=F32) + br_ref[d][:, sl])
                ig = sigmoid(jnp.dot(cvb[:, sl], wi_ref[d, gi], preferred_element_type=F32) + bi_ref[d][:, sl])
                log_a = (-LRU_C * rg) * sp[d][:, sl]
                a_ref[d, r0:r0 + rc, sl] = jnp.exp(log_a)
                th = jnp.tanh(log_a)
                b_ref[d, r0:r0 + rc, sl] = jnp.sqrt(-2.0 * th / (1.0 - th)) * (ig * cv[:, sl])

    for d in range(2):
        reverse = d == 1
        h0 = jnp.broadcast_to(h0_ref[0, d:d + 1, :], (SUBLANES, BRANCH_W))
        if not reverse:
            def emit(r, hh):
                hs_ref[pl.ds(r, SUBLANES), :] = hh
        else:
            def emit(r, hh):
                tot = hs_ref[pl.ds(r, SUBLANES), :] + hh
                y_ref[pl.ds(r, SUBLANES), :] = (tot * jax.nn.gelu(lg_ref[pl.ds(r, SUBLANES), :])).astype(BF16)
        h_end = _lru_scan(a_ref.at[d], b_ref.at[d], h0, nblk, reverse, emit)
        hl_ref[0, d:d + 1, :] = h_end[0:1, :]


def _lru(p, L, layer, conv_w, conv_b, w_rg, b_rg, w_ig, b_ig, lam, h0):
    nb = p.shape[0] // L
    gate_w = _layer_spec(layer, 2, N_GROUPS, GROUP_W, GROUP_W)
    dir_vec = _layer_spec(layer, 2, 1, BRANCH_W)
    return pl.pallas_call(
        functools.partial(_lru_body, L=L),
        grid=(nb,),
        in_specs=[
            pl.BlockSpec((L, BRANCH_W), lambda b: (b, S_LRU)),
            pl.BlockSpec((L, BRANCH_W), lambda b: (b, S_LG)),
            _layer_spec(layer, CONV_W, BRANCH_W),
            _layer_spec(layer, 1, BRANCH_W),
            gate_w, dir_vec, gate_w, dir_vec, dir_vec,
            pl.BlockSpec((1, 2, BRANCH_W), lambda b: (b, 0, 0)),
        ],
        out_specs=[
            pl.BlockSpec((L, BRANCH_W), lambda b: (b, 0)),
            pl.BlockSpec((1, 2, BRANCH_W), lambda b: (b, 0, 0)),
        ],
        out_shape=[
            jax.ShapeDtypeStruct((p.shape[0], BRANCH_W), BF16),
            jax.ShapeDtypeStruct((nb, 2, BRANCH_W), F32),
        ],
        scratch_shapes=[
            pltpu.VMEM((L + 2 * CONV_HALO, BRANCH_W), F32),
            pltpu.VMEM((2, L, BRANCH_W), F32),
            pltpu.VMEM((2, L, BRANCH_W), F32),
            pltpu.VMEM((L, BRANCH_W), F32),
        ],
        compiler_params=_params("parallel"),
        name="lru",
    )(p, p, conv_w, conv_b, w_rg, b_rg, w_ig, b_ig, lam, h0)


_NT = (((1,), (1,)), ((), ()))


def _softmax_pv(scores, values):
    m = scores[0].max(axis=-1, keepdims=True)
    for s in scores[1:]:
        m = jnp.maximum(m, s.max(axis=-1, keepdims=True))
    es = [jnp.exp(s - m) for s in scores]
    tot = es[0].sum(axis=-1, keepdims=True)
    for e in es[1:]:
        tot = tot + e.sum(axis=-1, keepdims=True)
    inv = 1.0 / tot
    out = None
    for e, v in zip(es, values):
        o = jnp.dot((e * inv).astype(BF16), v, preferred_element_type=F32)
        out = o if out is None else out + o
    return out


def _nattn_body(q_ref, k_ref, v_ref, kc_ref, vc_ref, t_ref, o_ref):
    rows = SEQ // GRID_W
    nq = NA_GROUP * GRID_W
    nk = NA_KEY_ROWS * GRID_W
    scale = GROUP_W ** -0.5
    kc = kc_ref[...]
    vc = vc_ref[...]

    def in_window(kr, row0):
        return ((kr >= row0) & (kr < row0 + NA_WIN_R)).astype(jnp.int32)

    def body(gi, carry):
        r0 = gi * NA_GROUP
        base = jnp.clip(r0 - NA_WIN_R // 2, 0, rows - NA_KEY_ROWS)
        qs = pl.multiple_of(r0 * GRID_W, nq)
        ks = pl.multiple_of(base * GRID_W, GRID_W)
        q = q_ref[pl.ds(qs, nq), :]
        kw = k_ref[pl.ds(ks, nk), :]
        vw = v_ref[pl.ds(ks, nk), :]
        s = lax.dot_general(q, kw, _NT, preferred_element_type=F32) * scale
        bias_rows = []
        for g in range(NA_GROUP):
            r = r0 + g
            row0 = jnp.clip(r - NA_WIN_R // 2, 0, rows - NA_WIN_R)
            pieces = []
            for jj in range(NA_KEY_ROWS // 2):
                kr = base + 2 * jj
                variant = in_window(kr, row0) + 2 * in_window(kr + 1, row0)
                e = jnp.clip(kr - r + NA_WIN_R, 0, NA_D - 1)
                pieces.append(t_ref[variant * NA_D + e])
            bias_rows.append(jnp.concatenate(pieces, axis=1))
        bias = jnp.concatenate(bias_rows, axis=0)
        s = jnp.where(bias > 0.5 * NEG_INF, s + bias, NEG_INF)
        sc = lax.dot_general(q, kc, _NT, preferred_element_type=F32) * scale
        o_ref[pl.ds(qs, nq), :] = _softmax_pv([s, sc], [vw, vc]).astype(BF16)
        return carry

    lax.fori_loop(0, rows // NA_GROUP, body, 0, unroll=2)


def _nattn(qkv, qkv_c, layer, bias_table):
    return pl.pallas_call(
        _nattn_body,
        grid=(BATCH, N_GROUPS),
        in_specs=[
            pl.BlockSpec((SEQ, GROUP_W), lambda b, h: (b, Q_COL + h)),
            pl.BlockSpec((SEQ, GROUP_W), lambda b, h: (b, K_COL + h)),
            pl.BlockSpec((SEQ, GROUP_W), lambda b, h: (b, V_COL + h)),
            pl.BlockSpec((CTX_LEN, GROUP_W), lambda b, h: (b, K_COL + h)),
            pl.BlockSpec((CTX_LEN, GROUP_W), lambda b, h: (b, V_COL + h)),
            pl.BlockSpec((None, None, 4 * NA_D, GRID_W, 2 * GRID_W), lambda b, h: (layer, h, 0, 0, 0)),
        ],
        out_specs=pl.BlockSpec((SEQ, GROUP_W), lambda b, h: (b, h)),
        out_shape=jax.ShapeDtypeStruct((qkv.shape[0], BRANCH_W), BF16),
        compiler_params=_params("parallel", "parallel"),
        name="nattn",
    )(qkv, qkv, qkv, qkv_c, qkv_c, bias_table)


def _cattn_body(q_ref, k_ref, v_ref, o_ref):
    scale = GROUP_W ** -0.5
    s = lax.dot_general(q_ref[...], k_ref[...], _NT, preferred_element_type=F32) * scale
    o_ref[...] = _softmax_pv([s], [v_ref[...]]).astype(BF16)


def _cattn(qkv_c):
    return pl.pallas_call(
        _cattn_body,
        grid=(BATCH, N_GROUPS),
        in_specs=[
            pl.BlockSpec((CTX_LEN, GROUP_W), lambda b, h: (b, Q_COL + h)),
            pl.BlockSpec((CTX_LEN, GROUP_W), lambda b, h: (b, K_COL + h)),
            pl.BlockSpec((CTX_LEN, GROUP_W), lambda b, h: (b, V_COL + h)),
        ],
        out_specs=pl.BlockSpec((CTX_LEN, GROUP_W), lambda b, h: (b, h)),
        out_shape=jax.ShapeDtypeStruct((qkv_c.shape[0], BRANCH_W), BF16),
        compiler_params=_params("parallel", "parallel"),
        name="cattn",
    )(qkv_c, qkv_c, qkv_c)


def _bias_table(rpb):
    col = jnp.arange(GRID_W)
    col_start = jnp.clip(col - NA_WIN_C // 2, 0, GRID_W - NA_WIN_C)
    col_ok = (col[None, :] >= col_start[:, None]) & (col[None, :] < col_start[:, None] + NA_WIN_C)
    pad = GRID_W - NA_WIN_C
    period = 2 * GRID_W - 1
    u = jnp.pad(rpb.astype(F32), [(0, 0)] * 3 + [(pad, pad)])
    skew = jnp.tile(u, GRID_W + 1)[..., :GRID_W * (period + 1)]
    skew = skew.reshape(rpb.shape[:3] + (GRID_W, period + 1))
    toe = skew[..., ::-1, :GRID_W]
    tab = jnp.where(col_ok, toe, NEG_INF)
    off = jnp.full_like(tab[:, :, :1], NEG_INF)
    lo = jnp.concatenate([off, tab], axis=2)
    hi = jnp.concatenate([tab, off], axis=2)
    none = jnp.full_like(lo, NEG_INF)
    variants = [jnp.concatenate([lo if v & 1 else none, hi if v & 2 else none], axis=-1) for v in range(4)]
    return jnp.concatenate(variants, axis=2)


def _merge_body(h_ref, y0_ref, y1_ref, y2_ref, y3_ref, g0_ref, g1_ref, g2_ref, g3_ref, wb_ref, o_ref):
    h = h_ref[...]
    acc = None
    for n, (y_ref, g_ref) in enumerate(((y0_ref, g0_ref), (y1_ref, g1_ref), (y2_ref, g2_ref), (y3_ref, g3_ref))):
        gate = jax.nn.sigmoid(jnp.dot(h, g_ref[...], preferred_element_type=F32))
        term = gate * jnp.dot(y_ref[...], wb_ref[n], preferred_element_type=F32)
        acc = term if acc is None else acc + term
    o_ref[...] = acc.astype(BF16)


def _merge(h, ys, layer, w_gate, w_branch):
    rows = h.shape[0]
    tm, tn = TM_MERGE, 512
    nj = D_MODEL // tn
    gate_spec = lambda n: pl.BlockSpec((None, D_MODEL, tn), lambda i, j: (layer, 0, n * nj + j))
    y_spec = pl.BlockSpec((tm, BRANCH_W), lambda i, j: (i, 0))
    return pl.pallas_call(
        _merge_body,
        grid=(rows // tm, nj),
        in_specs=[pl.BlockSpec((tm, D_MODEL), lambda i, j: (i, 0)), y_spec, y_spec, y_spec, y_spec,
                  gate_spec(0), gate_spec(1), gate_spec(2), gate_spec(3),
                  pl.BlockSpec((None, N_BRANCH, BRANCH_W, tn), lambda i, j: (layer, 0, 0, j))],
        out_specs=pl.BlockSpec((tm, tn), lambda i, j: (i, j)),
        out_shape=jax.ShapeDtypeStruct((rows, D_MODEL), BF16),
        compiler_params=_params("parallel", "arbitrary"),
        name="merge",
    )(h, *ys, w_gate, w_gate, w_gate, w_gate, w_branch)


def _outproj_body(m_ref, x_ref, w_ref, gt_ref, g_ref, sh_ref, sc_ref, xm_ref, h2_ref, *, tm):
    xm_ref[...] = x_ref[...] + gt_ref[...] * jnp.dot(m_ref[...], w_ref[...], preferred_element_type=F32)
    _rms_mod_rows(xm_ref, h2_ref, g_ref, sc_ref, sh_ref, tm)


def _outproj(merged, x, mods, layer, row_of_tile, w_out, g_ffn):
    rows = x.shape[0]
    tm = TM_OUT
    row_spec = pl.BlockSpec((tm, D_MODEL), lambda i: (i, 0))
    return pl.pallas_call(
        functools.partial(_outproj_body, tm=tm),
        grid=(rows // tm,),
        in_specs=[row_spec, row_spec,
                  _layer_spec(layer, D_MODEL, D_MODEL),
                  _mod_spec(layer, 2, row_of_tile(tm)),
                  _layer_spec(layer, 1, D_MODEL),
                  _mod_spec(layer, 3, row_of_tile(tm)),
                  _mod_spec(layer, 4, row_of_tile(tm))],
        out_specs=[row_spec, row_spec],
        out_shape=[jax.ShapeDtypeStruct((rows, D_MODEL), F32),
                   jax.ShapeDtypeStruct((rows, D_MODEL), BF16)],
        compiler_params=_params("parallel"),
        name="outproj",
    )(merged, x, w_out, mods, g_ffn, mods, mods)


def _ffn_body(h_ref, wg_ref, wu_ref, wd_ref, xm_ref, gt_ref, gf_ref, o_ref, a_ref, rs_ref, *, tm, tf, final):
    f = pl.program_id(1)

    h = h_ref[...]
    for s in range(tf // FFN_SUB):
        sl = slice(s * FFN_SUB, (s + 1) * FFN_SUB)
        g = jnp.dot(h, wg_ref[:, sl], preferred_element_type=F32)
        u = jnp.dot(h, wu_ref[:, sl], preferred_element_type=F32)
        a_ref[:, sl] = (jax.nn.silu(g) * u).astype(BF16)

    @pl.when(f == 0)
    def _():
        o_ref[...] = jnp.dot(a_ref[...], wd_ref[...], preferred_element_type=F32)

    @pl.when(f > 0)
    def _():
        o_ref[...] += jnp.dot(a_ref[...], wd_ref[...], preferred_element_type=F32)

    @pl.when(f == pl.num_programs(1) - 1)
    def _():
        def residual(c, carry):
            r = pl.multiple_of(c * NORM_ROWS, NORM_ROWS)
            y = xm_ref[pl.ds(r, NORM_ROWS), :] + gt_ref[...] * o_ref[pl.ds(r, NORM_ROWS), :]
            o_ref[pl.ds(r, NORM_ROWS), :] = y
            if final:
                ms = jnp.mean(y * y, axis=-1, keepdims=True)
                rs_ref[pl.ds(r, NORM_ROWS), :] = jnp.broadcast_to(lax.rsqrt(ms + EPS), (NORM_ROWS, LANES))
            return carry
        lax.fori_loop(0, tm // NORM_ROWS, residual, 0, unroll=4)

        if final:
            def scale(c, carry):
                r = pl.multiple_of(c * NORM_ROWS, NORM_ROWS)
                rs = jnp.concatenate([rs_ref[pl.ds(r, NORM_ROWS), :]] * (D_MODEL // LANES), axis=1)
                o_ref[pl.ds(r, NORM_ROWS), :] = (o_ref[pl.ds(r, NORM_ROWS), :] * rs) * gf_ref[...]
                return carry
            lax.fori_loop(0, tm // NORM_ROWS, scale, 0, unroll=4)


def _ffn(h2, xm, mods, layer, row_of_tile, w_g, w_u, w_d, g_final, final):
    rows = xm.shape[0]
    tm, tf = TM_FFN, 512
    row_spec = pl.BlockSpec((tm, D_MODEL), lambda i, f: (i, 0))
    return pl.pallas_call(
        functools.partial(_ffn_body, tm=tm, tf=tf, final=final),
        grid=(rows // tm, D_FF // tf),
        in_specs=[row_spec,
                  pl.BlockSpec((None, D_MODEL, tf), lambda i, f: (layer, 0, f)),
                  pl.BlockSpec((None, D_MODEL, tf), lambda i, f: (layer, 0, f)),
                  pl.BlockSpec((None, tf, D_MODEL), lambda i, f: (layer, f, 0)),
                  row_spec,
                  _mod_spec(layer, 5, row_of_tile(tm)),
                  pl.BlockSpec((1, D_MODEL), lambda i, f: (0, 0))],
        out_specs=row_spec,
        out_shape=jax.ShapeDtypeStruct((rows, D_MODEL), F32),
        scratch_shapes=[pltpu.VMEM((tm, tf), BF16), pltpu.VMEM((tm, LANES), F32)],
        compiler_params=_params("parallel", "arbitrary"),
        name="ffn",
    )(h2, w_g, w_u, w_d, xm, mods, g_final)


def kernel(x, c, ctx, c_ctx, w_mod, b_mod, g_mix, g_ffn, g_final, w_in, w_pool, pool_scale, gmlp_ln_g, w_sp, b_sp, conv_w, conv_b, w_rg, b_rg, w_ig, b_ig, lru_lam, rpb, w_branch, w_gate, w_out, w_ffn_gate, w_ffn_up, w_ffn_down):
    B, L, D = x.shape
    C = ctx.shape[1]
    assert (B, L, D, C) == (BATCH, SEQ, D_MODEL, CTX_LEN)
    xs = x.reshape(B * L, D)
    cs = ctx.reshape(B * C, D)

    cond = jnp.concatenate([c, c_ctx[None, :], jnp.zeros((COND_ROWS - B - 1, D), F32)], axis=0)
    mods = _adaln_all(cond, w_mod, b_mod)

    w_in_b = w_in.astype(BF16)
    w_gate_b = w_gate.astype(BF16).reshape(DEPTH, D, N_BRANCH * D)
    w_branch_b = w_branch.astype(BF16)
    w_out_b = w_out.astype(BF16)
    w_fg, w_fu, w_fd = w_ffn_gate.astype(BF16), w_ffn_up.astype(BF16), w_ffn_down.astype(BF16)
    g_mix_r = g_mix.reshape(DEPTH, 1, D)
    g_ffn_r = g_ffn.reshape(DEPTH, 1, D)
    g_fin_r = g_final.reshape(1, D)
    pool_args = (w_pool.astype(BF16), pool_scale.reshape(DEPTH, 1, BRANCH_W))
    sgu_args = (gmlp_ln_g.reshape(DEPTH, 1, BRANCH_W), w_sp.astype(BF16), b_sp.reshape(DEPTH, N_GROUPS, CHUNK, 1))
    lru_args = (conv_w, conv_b.reshape(DEPTH, 1, BRANCH_W),
                w_rg.astype(BF16), b_rg.reshape(DEPTH, 2, 1, BRANCH_W),
                w_ig.astype(BF16), b_ig.reshape(DEPTH, 2, 1, BRANCH_W),
                lru_lam.reshape(DEPTH, 2, 1, BRANCH_W))
    bias_table = _bias_table(rpb)
    h_zero = jnp.zeros((B, 2, BRANCH_W), F32)

    for l in range(DEPTH):
        last = l == DEPTH - 1
        px, qkv, hx = _inproj(xs, mods, l, _latent_row, g_mix_r, w_in_b)
        pc, qkv_c, hc = _inproj(cs, mods, l, _context_row, g_mix_r, w_in_b)

        yc_lru, hc_state = _lru(pc, C, l, *lru_args, h_zero)

        y_pool = _pool(px, L, l, *pool_args)
        y_sgu = _sgu(px, L, l, *sgu_args)
        y_lru, _ = _lru(px, L, l, *lru_args, hc_state)
        y_na = _nattn(qkv, qkv_c, l, bias_table)
        merged = _merge(hx, (y_pool, y_sgu, y_lru, y_na), l, w_gate_b, w_branch_b)
        xm, hx2 = _outproj(merged, xs, mods, l, _latent_row, w_out_b, g_ffn_r)

        if not last:
            yc_pool = _pool(pc, C, l, *pool_args)
            yc_sgu = _sgu(pc, C, l, *sgu_args)
            yc_na = _cattn(qkv_c)
            merged_c = _merge(hc, (yc_pool, yc_sgu, yc_lru, yc_na), l, w_gate_b, w_branch_b)
            cm, hc2 = _outproj(merged_c, cs, mods, l, _context_row, w_out_b, g_ffn_r)
            cs = _ffn(hc2, cm, mods, l, _context_row, w_fg, w_fu, w_fd, g_fin_r, False)

        xs = _ffn(hx2, xm, mods, l, _latent_row, w_fg, w_fu, w_fd, g_fin_r, last)

    return xs.reshape(B, L, D)
```

```python
import functools

import jax
import jax.numpy as jnp
from jax import lax
from jax.experimental import pallas as pl
from jax.experimental.pallas import tpu as pltpu

D_MODEL = 2048
BATCH = 8
SEQ = 2048
DEPTH = 2
CTX_LEN = 256
GRID_W = 64
N_BRANCH = 4
BRANCH_W = D_MODEL // N_BRANCH
GROUP_W = 128
N_GROUPS = BRANCH_W // GROUP_W
N_IN_SLICES = 8
IN_W = N_IN_SLICES * BRANCH_W
POOL_WINDOWS = (2, 4, 8, 16)
CHUNK = 128
CONV_W = 4
LRU_C = 8.0
NA_WIN_R = 8
NA_WIN_C = 16
D_FF = 5632
EPS = 1e-6
NEG_INF = -1e30

N_MOD = 6
COND_ROWS = 16
CTX_COND_ROW = BATCH
SUBLANES = 8
LANES = 128
POOL_HALO = 16
CONV_HALO = 8
VMEM_LIMIT = 60 * 1024 * 1024

F32 = jnp.float32
BF16 = jnp.bfloat16

S_POOL, S_U, S_V, S_LRU, S_LG, S_Q, S_K, S_VAL = range(N_IN_SLICES)
N_F32_SLICES = S_Q
HEADS_PER_SLICE = BRANCH_W // GROUP_W
Q_COL, K_COL, V_COL = (HEADS_PER_SLICE * (s - N_F32_SLICES) for s in (S_Q, S_K, S_VAL))
NA_GROUP = 4
NA_KEY_ROWS = 12
NA_D = 2 * NA_WIN_R
FFN_SUB = 256
NORM_ROWS = 16

TM_IN, TM_MERGE, TM_OUT, TM_FFN = 1024, 1024, 512, 1024


def _params(*sem):
    return pltpu.CompilerParams(dimension_semantics=sem, vmem_limit_bytes=VMEM_LIMIT)


def _layer_spec(layer, *shape):
    zeros = (0,) * len(shape)
    return pl.BlockSpec((None,) + shape, lambda *_: (layer,) + zeros)


def _mod_spec(layer, which, row_of_tile):
    def index(i, *_):
        return ((layer * COND_ROWS + row_of_tile(i)) * N_MOD + which, 0, 0)
    return pl.BlockSpec((None, 1, D_MODEL), index)


def _latent_row(tm):
    return lambda i: i // (SEQ // tm)


def _context_row(tm):
    return lambda i: CTX_COND_ROW


def _rms_mod_rows(src_ref, dst_ref, g_ref, sc_ref, sh_ref, gs_ref, rows):
    gs_ref[...] = g_ref[...] * (1.0 + sc_ref[...])

    def body(c, carry):
        r = pl.multiple_of(c * NORM_ROWS, NORM_ROWS)
        x = src_ref[pl.ds(r, NORM_ROWS), :]
        ms = jnp.mean(x * x, axis=-1, keepdims=True)
        y = (x * lax.rsqrt(ms + EPS)) * gs_ref[...] + sh_ref[...]
        dst_ref[pl.ds(r, NORM_ROWS), :] = y.astype(dst_ref.dtype)
        return carry
    lax.fori_loop(0, rows // NORM_ROWS, body, 0, unroll=4)


def _mod_body(c_ref, w_ref, b_ref, o_ref):
    @pl.when(pl.program_id(1) == 0)
    def _():
        o_ref[...] = jnp.broadcast_to(b_ref[...], o_ref.shape)
    a = jax.nn.silu(c_ref[...]).astype(BF16)
    o_ref[...] += jnp.dot(a, w_ref[...].astype(BF16), preferred_element_type=F32)


def _adaln_all(cond, w_mod, b_mod):
    tk = 256
    n = N_MOD * D_MODEL
    out = pl.pallas_call(
        _mod_body,
        grid=(DEPTH, D_MODEL // tk),
        in_specs=[
            pl.BlockSpec((COND_ROWS, tk), lambda l, k: (0, k)),
            pl.BlockSpec((None, tk, n), lambda l, k: (l, k, 0)),
            pl.BlockSpec((None, 1, n), lambda l, k: (l, 0, 0)),
        ],
        out_specs=pl.BlockSpec((None, COND_ROWS, n), lambda l, k: (l, 0, 0)),
        out_shape=jax.ShapeDtypeStruct((DEPTH, COND_ROWS, n), F32),
        compiler_params=_params("parallel", "arbitrary"),
        name="adaln",
    )(cond, w_mod, b_mod.reshape(DEPTH, 1, n))
    return out.reshape(DEPTH * COND_ROWS * N_MOD, 1, D_MODEL)


def _pick(j, values):
    out = values[-1]
    for t in range(len(values) - 2, -1, -1):
        out = jnp.where(j == t, values[t], out)
    return out


def _inproj_body(x_ref, sh_ref, sc_ref, g_ref, w_ref, p_ref, qkv_ref, h_ref, gs_ref, *, tm, slices):
    j = pl.program_id(1)
    s = _pick(j, slices)

    @pl.when(j == 0)
    def _():
        _rms_mod_rows(x_ref, h_ref, g_ref, sc_ref, sh_ref, gs_ref, tm)

    @pl.when(s < N_F32_SLICES)
    def _():
        p_ref[...] = jnp.dot(h_ref[...], w_ref[...], preferred_element_type=F32)

    @pl.when(s >= N_F32_SLICES)
    def _():
        qkv_ref[...] = jnp.dot(h_ref[...], w_ref[...], preferred_element_type=F32).astype(BF16)


def _inproj(x, mods, layer, row_of_tile, g_mix, w_in, slices=tuple(range(N_IN_SLICES))):
    rows = x.shape[0]
    tm, tn = TM_IN, BRANCH_W
    f32_blocks = [s for s in slices if s < N_F32_SLICES]
    bf16_blocks = [s - N_F32_SLICES for s in slices if s >= N_F32_SLICES]
    p_block = tuple(s if s < N_F32_SLICES else f32_blocks[-1] for s in slices)
    qkv_block = tuple(s - N_F32_SLICES if s >= N_F32_SLICES else bf16_blocks[0] for s in slices)
    return pl.pallas_call(
        functools.partial(_inproj_body, tm=tm, slices=slices),
        grid=(rows // tm, len(slices)),
        in_specs=[
            pl.BlockSpec((tm, D_MODEL), lambda i, j: (i, 0)),
            _mod_spec(layer, 0, row_of_tile(tm)),
            _mod_spec(layer, 1, row_of_tile(tm)),
            _layer_spec(layer, 1, D_MODEL),
            pl.BlockSpec((None, D_MODEL, tn), lambda i, j: (layer, 0, _pick(j, slices))),
        ],
        out_specs=[
            pl.BlockSpec((tm, tn), lambda i, j: (i, _pick(j, p_block))),
            pl.BlockSpec((tm, tn), lambda i, j: (i, _pick(j, qkv_block))),
            pl.BlockSpec((tm, D_MODEL), lambda i, j: (i, 0)),
        ],
        out_shape=[
            jax.ShapeDtypeStruct((rows, N_F32_SLICES * BRANCH_W), F32),
            jax.ShapeDtypeStruct((rows, (N_IN_SLICES - N_F32_SLICES) * BRANCH_W), BF16),
            jax.ShapeDtypeStruct((rows, D_MODEL), BF16),
        ],
        scratch_shapes=[pltpu.VMEM((1, D_MODEL), F32)],
        compiler_params=_params("parallel", "arbitrary"),
        name="inproj",
    )(x, mods, mods, g_mix, w_in)


def _pool_body(z_ref, w_ref, s_ref, o_ref, zp_ref, *, L):
    halo = jnp.zeros((POOL_HALO, BRANCH_W), F32)
    zp_ref[0:POOL_HALO, :] = halo
    zp_ref[POOL_HALO + L:2 * POOL_HALO + L, :] = halo
    zp_ref[POOL_HALO:POOL_HALO + L, :] = z_ref[...]
    rc = CHUNK
    for c in range(L // rc):
        r0 = c * rc
        t = r0 + lax.broadcasted_iota(jnp.int32, (rc, GROUP_W), 0)
        for gi, win in enumerate(POOL_WINDOWS):
            half = win // 2
            sl = slice(gi * GROUP_W, (gi + 1) * GROUP_W)
            acc = zp_ref[POOL_HALO + r0 - half:POOL_HALO + r0 - half + rc, sl]
            for o in range(-half + 1, half):
                acc = acc + zp_ref[POOL_HALO + r0 + o:POOL_HALO + r0 + o + rc, sl]
            cnt = (jnp.minimum(t + half, L) - jnp.maximum(t - half, 0)).astype(F32)
            pooled = acc / cnt - z_ref[r0:r0 + rc, sl]
            y = jnp.dot(pooled.astype(BF16), w_ref[gi], preferred_element_type=F32)
            o_ref[r0:r0 + rc, sl] = (y * s_ref[:, sl]).astype(BF16)


def _pool(p, L, layer, w_pool, pool_scale):
    nb = p.shape[0] // L
    return pl.pallas_call(
        functools.partial(_pool_body, L=L),
        grid=(nb,),
        in_specs=[
            pl.BlockSpec((L, BRANCH_W), lambda b: (b, S_POOL)),
            _layer_spec(layer, N_GROUPS, GROUP_W, GROUP_W),
            _layer_spec(layer, 1, BRANCH_W),
        ],
        out_specs=pl.BlockSpec((L, BRANCH_W), lambda b: (b, 0)),
        out_shape=jax.ShapeDtypeStruct((p.shape[0], BRANCH_W), BF16),
        scratch_shapes=[pltpu.VMEM((L + 2 * POOL_HALO, BRANCH_W), F32)],
        compiler_params=_params("parallel"),
        name="pool",
    )(p, w_pool, pool_scale)


def _sgu_body(u_ref, v_ref, g_ref, w_ref, b_ref, o_ref, *, L):
    for n in range(L // CHUNK):
        r0 = n * CHUNK
        v = v_ref[r0:r0 + CHUNK, :]
        vc = v - jnp.mean(v, axis=-1, keepdims=True)
        var = jnp.mean(vc * vc, axis=-1, keepdims=True)
        vn = ((vc * lax.rsqrt(var + EPS)) * g_ref[...]).astype(BF16)
        for gi in range(N_GROUPS):
            sl = slice(gi * GROUP_W, (gi + 1) * GROUP_W)
            mixed = jnp.dot(w_ref[gi], vn[:, sl], preferred_element_type=F32) + b_ref[gi]
            o_ref[r0:r0 + CHUNK, sl] = (u_ref[r0:r0 + CHUNK, sl] * mixed).astype(BF16)


def _sgu(p, L, layer, ln_g, w_sp, b_sp):
    nb = p.shape[0] // L
    return pl.pallas_call(
        functools.partial(_sgu_body, L=L),
        grid=(nb,),
        in_specs=[
            pl.BlockSpec((L, BRANCH_W), lambda b: (b, S_U)),
            pl.BlockSpec((L, BRANCH_W), lambda b: (b, S_V)),
            _layer_spec(layer, 1, BRANCH_W),
            _layer_spec(layer, N_GROUPS, CHUNK, CHUNK),
            _layer_spec(layer, N_GROUPS, CHUNK, 1),
        ],
        out_specs=pl.BlockSpec((L, BRANCH_W), lambda b: (b, 0)),
        out_shape=jax.ShapeDtypeStruct((p.shape[0], BRANCH_W), BF16),
        compiler_params=_params("parallel"),
        name="sgu",
    )(p, p, ln_g, w_sp, b_sp)


def _lru_scan(a_ref, b_ref, h0, nblk, reverse, emit):
    row = lax.broadcasted_iota(jnp.int32, (SUBLANES, BRANCH_W), 0)

    def body(jj, h):
        j = (nblk - 1 - jj) if reverse else jj
        r = pl.multiple_of(j * SUBLANES, SUBLANES)
        a = a_ref[pl.ds(r, SUBLANES), :]
        b = b_ref[pl.ds(r, SUBLANES), :]
        for s in (1, 2, 4):
            shift = (SUBLANES - s) if reverse else s
            a_s = pltpu.roll(a, shift, 0)
            b_s = pltpu.roll(b, shift, 0)
            m = (row < SUBLANES - s) if reverse else (row >= s)
            b = jnp.where(m, a * b_s + b, b)
            a = jnp.where(m, a * a_s, a)
        hh = a * h + b
        emit(r, hh)
        edge = hh[0:1, :] if reverse else hh[SUBLANES - 1:SUBLANES, :]
        return jnp.broadcast_to(edge, (SUBLANES, BRANCH_W))

    return lax.fori_loop(0, nblk, body, h0, unroll=4)


def _lru_body(z_ref, lg_ref, cw_ref, cb_ref, wr_ref, br_ref, wi_ref, bi_ref, lam_ref, h0_ref,
              y_ref, hl_ref, zp_ref, a_ref, b_ref, hs_ref, *, L):
    halo = jnp.zeros((CONV_HALO, BRANCH_W), F32)
    zp_ref[0:CONV_HALO, :] = halo
    zp_ref[CONV_HALO + L:2 * CONV_HALO + L, :] = halo
    zp_ref[CONV_HALO:CONV_HALO + L, :] = z_ref[...]
    left = CONV_W // 2
    rc = CHUNK
    nblk = L // SUBLANES

    def conv_chunk(r0):
        out = zp_ref[CONV_HALO + r0 - left:CONV_HALO + r0 - left + rc, :] * cw_ref[0:1, :] + cb_ref[...]
        for k in range(1, CONV_W):
            lo = CONV_HALO + r0 - left + k
            out = out + zp_ref[lo:lo + rc, :] * cw_ref[k:k + 1, :]
        return out

    def sigmoid(t):
        return 0.5 * jnp.tanh(0.5 * t) + 0.5

    sp = [jax.nn.softplus(-lam_ref[d]) for d in range(2)]
    for c in range(L // rc):
        r0 = c * rc
        cv = conv_chunk(r0)
        cvb = cv.astype(BF16)
        for d in range(2):
            for gi in range(N_GROUPS):
                sl = slice(gi * GROUP_W, (gi + 1) * GROUP_W)
                rg = sigmoid(jnp.dot(cvb[:, sl], wr_ref[d, gi], preferred_element_type=F32) + br_ref[d][:, sl])
                ig = sigmoid(jnp.dot(cvb[:, sl], wi_ref[d, gi], preferred_element_type=F32) + bi_ref[d][:, sl])
                log_a = (-LRU_C * rg) * sp[d][:, sl]
                a_ref[d, r0:r0 + rc, sl] = jnp.exp(log_a)
                th = jnp.tanh(log_a)
                b_ref[d, r0:r0 + rc, sl] = jnp.sqrt(-2.0 * th / (1.0 - th)) * (ig * cv[:, sl])

    for d in range(2):
        reverse = d == 1
        h0 = jnp.broadcast_to(h0_ref[0, d:d + 1, :], (SUBLANES, BRANCH_W))
        if not reverse:
            def emit(r, hh):
                hs_ref[pl.ds(r, SUBLANES), :] = hh
        else:
            def emit(r, hh):
                tot = hs_ref[pl.ds(r, SUBLANES), :] + hh
                y_ref[pl.ds(r, SUBLANES), :] = (tot * jax.nn.gelu(lg_ref[pl.ds(r, SUBLANES), :])).astype(BF16)
        h_end = _lru_scan(a_ref.at[d], b_ref.at[d], h0, nblk, reverse, emit)
        hl_ref[0, d:d + 1, :] = h_end[0:1, :]


def _lru(p, L, layer, conv_w, conv_b, w_rg, b_rg, w_ig, b_ig, lam, h0):
    nb = p.shape[0] // L
    gate_w = _layer_spec(layer, 2, N_GROUPS, GROUP_W, GROUP_W)
    dir_vec = _layer_spec(layer, 2, 1, BRANCH_W)
    return pl.pallas_call(
        functools.partial(_lru_body, L=L),
        grid=(nb,),
        in_specs=[
            pl.BlockSpec((L, BRANCH_W), lambda b: (b, S_LRU)),
            pl.BlockSpec((L, BRANCH_W), lambda b: (b, S_LG)),
            _layer_spec(layer, CONV_W, BRANCH_W),
            _layer_spec(layer, 1, BRANCH_W),
            gate_w, dir_vec, gate_w, dir_vec, dir_vec,
            pl.BlockSpec((1, 2, BRANCH_W), lambda b: (b, 0, 0)),
        ],
        out_specs=[
            pl.BlockSpec((L, BRANCH_W), lambda b: (b, 0)),
            pl.BlockSpec((1, 2, BRANCH_W), lambda b: (b, 0, 0)),
        ],
        out_shape=[
            jax.ShapeDtypeStruct((p.shape[0], BRANCH_W), BF16),
            jax.ShapeDtypeStruct((nb, 2, BRANCH_W), F32),
        ],
        scratch_shapes=[
            pltpu.VMEM((L + 2 * CONV_HALO, BRANCH_W), F32),
            pltpu.VMEM((2, L, BRANCH_W), F32),
            pltpu.VMEM((2, L, BRANCH_W), F32),
            pltpu.VMEM((L, BRANCH_W), F32),
        ],
        compiler_params=_params("parallel"),
        name="lru",
    )(p, p, conv_w, conv_b, w_rg, b_rg, w_ig, b_ig, lam, h0)


_NT = (((1,), (1,)), ((), ()))


def _softmax_pv(scores, values):
    m = scores[0].max(axis=-1, keepdims=True)
    for s in scores[1:]:
        m = jnp.maximum(m, s.max(axis=-1, keepdims=True))
    es = [jnp.exp(s - m) for s in scores]
    tot = es[0].sum(axis=-1, keepdims=True)
    for e in es[1:]:
        tot = tot + e.sum(axis=-1, keepdims=True)
    inv = 1.0 / tot
    out = None
    for e, v in zip(es, values):
        o = jnp.dot((e * inv).astype(BF16), v, preferred_element_type=F32)
        out = o if out is None else out + o
    return out


def _nattn_body(q_ref, k_ref, v_ref, kc_ref, vc_ref, t_ref, o_ref):
    rows = SEQ // GRID_W
    nq = NA_GROUP * GRID_W
    nk = NA_KEY_ROWS * GRID_W
    scale = GROUP_W ** -0.5
    kc = kc_ref[...]
    vc = vc_ref[...]

    def in_window(kr, row0):
        return ((kr >= row0) & (kr < row0 + NA_WIN_R)).astype(jnp.int32)

    def body(gi, carry):
        r0 = gi * NA_GROUP
        base = jnp.clip(r0 - NA_WIN_R // 2, 0, rows - NA_KEY_ROWS)
        qs = pl.multiple_of(r0 * GRID_W, nq)
        ks = pl.multiple_of(base * GRID_W, GRID_W)
        q = q_ref[pl.ds(qs, nq), :]
        kw = k_ref[pl.ds(ks, nk), :]
        vw = v_ref[pl.ds(ks, nk), :]
        s = lax.dot_general(q, kw, _NT, preferred_element_type=F32) * scale
        bias_rows = []
        for g in range(NA_GROUP):
            r = r0 + g
            row0 = jnp.clip(r - NA_WIN_R // 2, 0, rows - NA_WIN_R)
            pieces = []
            for jj in range(NA_KEY_ROWS // 2):
                kr = base + 2 * jj
                variant = in_window(kr, row0) + 2 * in_window(kr + 1, row0)
                e = jnp.clip(kr - r + NA_WIN_R, 0, NA_D - 1)
                pieces.append(t_ref[variant * NA_D + e])
            bias_rows.append(jnp.concatenate(pieces, axis=1))
        bias = jnp.concatenate(bias_rows, axis=0)
        s = jnp.where(bias > 0.5 * NEG_INF, s + bias, NEG_INF)
        sc = lax.dot_general(q, kc, _NT, preferred_element_type=F32) * scale
        o_ref[pl.ds(qs, nq), :] = _softmax_pv([s, sc], [vw, vc]).astype(BF16)
        return carry

    lax.fori_loop(0, rows // NA_GROUP, body, 0, unroll=2)


def _nattn(qkv, qkv_c, layer, bias_table):
    return pl.pallas_call(
        _nattn_body,
        grid=(BATCH, N_GROUPS),
        in_specs=[
            pl.BlockSpec((SEQ, GROUP_W), lambda b, h: (b, Q_COL + h)),
            pl.BlockSpec((SEQ, GROUP_W), lambda b, h: (b, K_COL + h)),
            pl.BlockSpec((SEQ, GROUP_W), lambda b, h: (b, V_COL + h)),
            pl.BlockSpec((CTX_LEN, GROUP_W), lambda b, h: (b, K_COL + h)),
            pl.BlockSpec((CTX_LEN, GROUP_W), lambda b, h: (b, V_COL + h)),
            pl.BlockSpec((None, None, 4 * NA_D, GRID_W, 2 * GRID_W), lambda b, h: (layer, h, 0, 0, 0)),
        ],
        out_specs=pl.BlockSpec((SEQ, GROUP_W), lambda b, h: (b, h)),
        out_shape=jax.ShapeDtypeStruct((qkv.shape[0], BRANCH_W), BF16),
        compiler_params=_params("parallel", "parallel"),
        name="nattn",
    )(qkv, qkv, qkv, qkv_c, qkv_c, bias_table)


def _cattn_body(q_ref, k_ref, v_ref, o_ref):
    scale = GROUP_W ** -0.5
    s = lax.dot_general(q_ref[...], k_ref[...], _NT, preferred_element_type=F32) * scale
    o_ref[...] = _softmax_pv([s], [v_ref[...]]).astype(BF16)


def _cattn(qkv_c):
    return pl.pallas_call(
        _cattn_body,
        grid=(BATCH, N_GROUPS),
        in_specs=[
            pl.BlockSpec((CTX_LEN, GROUP_W), lambda b, h: (b, Q_COL + h)),
            pl.BlockSpec((CTX_LEN, GROUP_W), lambda b, h: (b, K_COL + h)),
            pl.BlockSpec((CTX_LEN, GROUP_W), lambda b, h: (b, V_COL + h)),
        ],
        out_specs=pl.BlockSpec((CTX_LEN, GROUP_W), lambda b, h: (b, h)),
        out_shape=jax.ShapeDtypeStruct((qkv_c.shape[0], BRANCH_W), BF16),
        compiler_params=_params("parallel", "parallel"),
        name="cattn",
    )(qkv_c, qkv_c, qkv_c)


def _bias_table(rpb):
    col = jnp.arange(GRID_W)
    col_start = jnp.clip(col - NA_WIN_C // 2, 0, GRID_W - NA_WIN_C)
    col_ok = (col[None, :] >= col_start[:, None]) & (col[None, :] < col_start[:, None] + NA_WIN_C)
    pad = GRID_W - NA_WIN_C
    period = 2 * GRID_W - 1
    u = jnp.pad(rpb.astype(F32), [(0, 0)] * 3 + [(pad, pad)])
    skew = jnp.tile(u, GRID_W + 1)[..., :GRID_W * (period + 1)]
    skew = skew.reshape(rpb.shape[:3] + (GRID_W, period + 1))
    toe = skew[..., ::-1, :GRID_W]
    tab = jnp.where(col_ok, toe, NEG_INF)
    off = jnp.full_like(tab[:, :, :1], NEG_INF)
    lo = jnp.concatenate([off, tab], axis=2)
    hi = jnp.concatenate([tab, off], axis=2)
    none = jnp.full_like(lo, NEG_INF)
    variants = [jnp.concatenate([lo if v & 1 else none, hi if v & 2 else none], axis=-1) for v in range(4)]
    return jnp.concatenate(variants, axis=2)


def _merge_body(h_ref, y0_ref, y1_ref, y2_ref, y3_ref, g0_ref, g1_ref, g2_ref, g3_ref, wb_ref, o_ref):
    h = h_ref[...]
    acc = None
    for n, (y_ref, g_ref) in enumerate(((y0_ref, g0_ref), (y1_ref, g1_ref), (y2_ref, g2_ref), (y3_ref, g3_ref))):
        gate = jax.nn.sigmoid(jnp.dot(h, g_ref[...], preferred_element_type=F32))
        term = gate * jnp.dot(y_ref[...], wb_ref[n], preferred_element_type=F32)
        acc = term if acc is None else acc + term
    o_ref[...] = acc.astype(BF16)


def _merge(h, ys, layer, w_gate, w_branch):
    rows = h.shape[0]
    tm, tn = TM_MERGE, 512
    nj = D_MODEL // tn
    gate_spec = lambda n: pl.BlockSpec((D_MODEL, tn), lambda i, j: (0, n * nj + j))
    y_spec = pl.BlockSpec((tm, BRANCH_W), lambda i, j: (i, 0))
    return pl.pallas_call(
        _merge_body,
        grid=(rows // tm, nj),
        in_specs=[pl.BlockSpec((tm, D_MODEL), lambda i, j: (i, 0)), y_spec, y_spec, y_spec, y_spec,
                  gate_spec(0), gate_spec(1), gate_spec(2), gate_spec(3),
                  pl.BlockSpec((None, N_BRANCH, BRANCH_W, tn), lambda i, j: (layer, 0, 0, j))],
        out_specs=pl.BlockSpec((tm, tn), lambda i, j: (i, j)),
        out_shape=jax.ShapeDtypeStruct((rows, D_MODEL), BF16),
        compiler_params=_params("parallel", "arbitrary"),
        name="merge",
    )(h, *ys, w_gate, w_gate, w_gate, w_gate, w_branch)


def _outproj_body(m_ref, x_ref, w_ref, gt_ref, g_ref, sh_ref, sc_ref, xm_ref, h2_ref, gs_ref, *, tm):
    xm_ref[...] = x_ref[...] + gt_ref[...] * jnp.dot(m_ref[...], w_ref[...], preferred_element_type=F32)
    _rms_mod_rows(xm_ref, h2_ref, g_ref, sc_ref, sh_ref, gs_ref, tm)


def _outproj(merged, x, mods, layer, row_of_tile, w_out, g_ffn):
    rows = x.shape[0]
    tm = TM_OUT
    row_spec = pl.BlockSpec((tm, D_MODEL), lambda i: (i, 0))
    return pl.pallas_call(
        functools.partial(_outproj_body, tm=tm),
        grid=(rows // tm,),
        in_specs=[row_spec, row_spec,
                  _layer_spec(layer, D_MODEL, D_MODEL),
                  _mod_spec(layer, 2, row_of_tile(tm)),
                  _layer_spec(layer, 1, D_MODEL),
                  _mod_spec(layer, 3, row_of_tile(tm)),
                  _mod_spec(layer, 4, row_of_tile(tm))],
        out_specs=[row_spec, row_spec],
        out_shape=[jax.ShapeDtypeStruct((rows, D_MODEL), F32),
                   jax.ShapeDtypeStruct((rows, D_MODEL), BF16)],
        scratch_shapes=[pltpu.VMEM((1, D_MODEL), F32)],
        compiler_params=_params("parallel"),
        name="outproj",
    )(merged, x, w_out, mods, g_ffn, mods, mods)


def _ffn_body(h_ref, wg_ref, wu_ref, wd_ref, xm_ref, gt_ref, gf_ref, o_ref, a_ref, rs_ref, *, tm, tf, final):
    f = pl.program_id(1)

    h = h_ref[...]
    for s in range(tf // FFN_SUB):
        sl = slice(s * FFN_SUB, (s + 1) * FFN_SUB)
        g = jnp.dot(h, wg_ref[:, sl], preferred_element_type=F32)
        u = jnp.dot(h, wu_ref[:, sl], preferred_element_type=F32)
        a_ref[:, sl] = (jax.nn.silu(g) * u).astype(BF16)

    @pl.when(f == 0)
    def _():
        o_ref[...] = jnp.dot(a_ref[...], wd_ref[...], preferred_element_type=F32)

    @pl.when(f > 0)
    def _():
        o_ref[...] += jnp.dot(a_ref[...], wd_ref[...], preferred_element_type=F32)

    @pl.when(f == pl.num_programs(1) - 1)
    def _():
        def residual(c, carry):
            r = pl.multiple_of(c * NORM_ROWS, NORM_ROWS)
            y = xm_ref[pl.ds(r, NORM_ROWS), :] + gt_ref[...] * o_ref[pl.ds(r, NORM_ROWS), :]
            o_ref[pl.ds(r, NORM_ROWS), :] = y
            if final:
                ms = jnp.mean(y * y, axis=-1, keepdims=True)
                rs_ref[pl.ds(r, NORM_ROWS), :] = jnp.broadcast_to(lax.rsqrt(ms + EPS), (NORM_ROWS, LANES))
            return carry
        lax.fori_loop(0, tm // NORM_ROWS, residual, 0, unroll=4)

        if final:
            def scale(c, carry):
                r = pl.multiple_of(c * NORM_ROWS, NORM_ROWS)
                rs = jnp.concatenate([rs_ref[pl.ds(r, NORM_ROWS), :]] * (D_MODEL // LANES), axis=1)
                o_ref[pl.ds(r, NORM_ROWS), :] = (o_ref[pl.ds(r, NORM_ROWS), :] * rs) * gf_ref[...]
                return carry
            lax.fori_loop(0, tm // NORM_ROWS, scale, 0, unroll=4)


def _ffn(h2, xm, mods, layer, row_of_tile, w_g, w_u, w_d, g_final, final):
    rows = xm.shape[0]
    tm, tf = TM_FFN, 512
    row_spec = pl.BlockSpec((tm, D_MODEL), lambda i, f: (i, 0))
    return pl.pallas_call(
        functools.partial(_ffn_body, tm=tm, tf=tf, final=final),
        grid=(rows // tm, D_FF // tf),
        in_specs=[row_spec,
                  pl.BlockSpec((None, D_MODEL, tf), lambda i, f: (layer, 0, f)),
                  pl.BlockSpec((None, D_MODEL, tf), lambda i, f: (layer, 0, f)),
                  pl.BlockSpec((None, tf, D_MODEL), lambda i, f: (layer, f, 0)),
                  row_spec,
                  _mod_spec(layer, 5, row_of_tile(tm)),
                  pl.BlockSpec((1, D_MODEL), lambda i, f: (0, 0))],
        out_specs=row_spec,
        out_shape=jax.ShapeDtypeStruct((rows, D_MODEL), F32),
        scratch_shapes=[pltpu.VMEM((tm, tf), BF16), pltpu.VMEM((tm, LANES), F32)],
        compiler_params=_params("parallel", "arbitrary"),
        name="ffn",
    )(h2, w_g, w_u, w_d, xm, mods, g_final)


def kernel(x, c, ctx, c_ctx, w_mod, b_mod, g_mix, g_ffn, g_final, w_in, w_pool, pool_scale, gmlp_ln_g, w_sp, b_sp, conv_w, conv_b, w_rg, b_rg, w_ig, b_ig, lru_lam, rpb, w_branch, w_gate, w_out, w_ffn_gate, w_ffn_up, w_ffn_down):
    B, L, D = x.shape
    C = ctx.shape[1]
    assert (B, L, D, C) == (BATCH, SEQ, D_MODEL, CTX_LEN)
    xs = x.reshape(B * L, D)
    cs = ctx.reshape(B * C, D)

    cond = jnp.concatenate([c, c_ctx[None, :], jnp.zeros((COND_ROWS - B - 1, D), F32)], axis=0)
    mods = _adaln_all(cond, w_mod, b_mod)

    w_in_b = w_in.astype(BF16)
    w_gate_b = [w_gate[l].astype(BF16).reshape(D, N_BRANCH * D) for l in range(DEPTH)]
    w_branch_b = w_branch.astype(BF16)
    w_out_b = w_out.astype(BF16)
    w_fg, w_fu, w_fd = w_ffn_gate.astype(BF16), w_ffn_up.astype(BF16), w_ffn_down.astype(BF16)
    g_mix_r = g_mix.reshape(DEPTH, 1, D)
    g_ffn_r = g_ffn.reshape(DEPTH, 1, D)
    g_fin_r = g_final.reshape(1, D)
    pool_args = (w_pool.astype(BF16), pool_scale.reshape(DEPTH, 1, BRANCH_W))
    sgu_args = (gmlp_ln_g.reshape(DEPTH, 1, BRANCH_W), w_sp.astype(BF16), b_sp.reshape(DEPTH, N_GROUPS, CHUNK, 1))
    lru_args = (conv_w, conv_b.reshape(DEPTH, 1, BRANCH_W),
                w_rg.astype(BF16), b_rg.reshape(DEPTH, 2, 1, BRANCH_W),
                w_ig.astype(BF16), b_ig.reshape(DEPTH, 2, 1, BRANCH_W),
                lru_lam.reshape(DEPTH, 2, 1, BRANCH_W))
    bias_table = _bias_table(rpb)
    h_zero = jnp.zeros((B, 2, BRANCH_W), F32)

    for l in range(DEPTH):
        last = l == DEPTH - 1
        px, qkv, hx = _inproj(xs, mods, l, _latent_row, g_mix_r, w_in_b)
        ctx_slices = (S_LRU, S_LG, S_K, S_VAL) if last else tuple(range(N_IN_SLICES))
        pc, qkv_c, hc = _inproj(cs, mods, l, _context_row, g_mix_r, w_in_b, ctx_slices)

        yc_lru, hc_state = _lru(pc, C, l, *lru_args, h_zero)

        y_pool = _pool(px, L, l, *pool_args)
        y_sgu = _sgu(px, L, l, *sgu_args)
        y_lru, _ = _lru(px, L, l, *lru_args, hc_state)
        y_na = _nattn(qkv, qkv_c, l, bias_table)
        merged = _merge(hx, (y_pool, y_sgu, y_lru, y_na), l, w_gate_b[l], w_branch_b)
        xm, hx2 = _outproj(merged, xs, mods, l, _latent_row, w_out_b, g_ffn_r)

        if not last:
            yc_pool = _pool(pc, C, l, *pool_args)
            yc_sgu = _sgu(pc, C, l, *sgu_args)
            yc_na = _cattn(qkv_c)
            merged_c = _merge(hc, (yc_pool, yc_sgu, yc_lru, yc_na), l, w_gate_b[l], w_branch_b)
            cm, hc2 = _outproj(merged_c, cs, mods, l, _context_row, w_out_b, g_ffn_r)
            cs = _ffn(hc2, cm, mods, l, _context_row, w_fg, w_fu, w_fd, g_fin_r, False)

        xs = _ffn(hx2, xm, mods, l, _latent_row, w_fg, w_fu, w_fd, g_fin_r, last)

    return xs.reshape(B, L, D)
```

```python
import functools

import jax
import jax.numpy as jnp
from jax import lax
from jax.experimental import pallas as pl
from jax.experimental.pallas import tpu as pltpu

D_MODEL = 2048
BATCH = 8
SEQ = 2048
DEPTH = 2
CTX_LEN = 256
GRID_W = 64
N_BRANCH = 4
BRANCH_W = D_MODEL // N_BRANCH
GROUP_W = 128
N_GROUPS = BRANCH_W // GROUP_W
N_IN_SLICES = 8
IN_W = N_IN_SLICES * BRANCH_W
POOL_WINDOWS = (2, 4, 8, 16)
CHUNK = 128
CONV_W = 4
LRU_C = 8.0
NA_WIN_R = 8
NA_WIN_C = 16
D_FF = 5632
EPS = 1e-6
NEG_INF = -1e30
LOG2_E = 1.4426950408889634

N_MOD = 6
COND_ROWS = 16
CTX_COND_ROW = BATCH
SUBLANES = 8
LANES = 128
POOL_HALO = 16
CONV_HALO = 8
VMEM_LIMIT = 60 * 1024 * 1024

F32 = jnp.float32
BF16 = jnp.bfloat16

S_POOL, S_U, S_V, S_LRU, S_LG, S_Q, S_K, S_VAL = range(N_IN_SLICES)
N_F32_SLICES = S_Q
HEADS_PER_SLICE = BRANCH_W // GROUP_W
Q_COL, K_COL, V_COL = (HEADS_PER_SLICE * (s - N_F32_SLICES) for s in (S_Q, S_K, S_VAL))
NA_GROUP = 4
NA_KEY_ROWS = 12
NA_D = 2 * NA_WIN_R
FFN_SUB = 256
NORM_ROWS = 16

TM_IN, TM_MERGE, TM_OUT, TM_FFN = 1024, 1024, 512, 1024


def _params(*sem):
    return pltpu.CompilerParams(dimension_semantics=sem, vmem_limit_bytes=VMEM_LIMIT)


def _layer_spec(layer, *shape):
    zeros = (0,) * len(shape)
    return pl.BlockSpec((None,) + shape, lambda *_: (layer,) + zeros)


def _mod_spec(layer, which, row_of_tile):
    def index(i, *_):
        return ((layer * COND_ROWS + row_of_tile(i)) * N_MOD + which, 0, 0)
    return pl.BlockSpec((None, 1, D_MODEL), index)


def _latent_row(tm):
    return lambda i: i // (SEQ // tm)


def _context_row(tm):
    return lambda i: CTX_COND_ROW


def _rms_mod_rows(src_ref, dst_ref, g_ref, sc_ref, sh_ref, gs_ref, rows):
    gs_ref[...] = g_ref[...] * (1.0 + sc_ref[...])

    def body(c, carry):
        r = pl.multiple_of(c * NORM_ROWS, NORM_ROWS)
        x = src_ref[pl.ds(r, NORM_ROWS), :]
        ms = jnp.mean(x * x, axis=-1, keepdims=True)
        y = (x * lax.rsqrt(ms + EPS)) * gs_ref[...] + sh_ref[...]
        dst_ref[pl.ds(r, NORM_ROWS), :] = y.astype(dst_ref.dtype)
        return carry
    lax.fori_loop(0, rows // NORM_ROWS, body, 0, unroll=4)


def _mod_body(c_ref, w_ref, b_ref, o_ref):
    @pl.when(pl.program_id(1) == 0)
    def _():
        o_ref[...] = jnp.broadcast_to(b_ref[...], o_ref.shape)
    a = jax.nn.silu(c_ref[...]).astype(BF16)
    o_ref[...] += jnp.dot(a, w_ref[...].astype(BF16), preferred_element_type=F32)


def _adaln_all(cond, w_mod, b_mod):
    tk = 256
    n = N_MOD * D_MODEL
    out = pl.pallas_call(
        _mod_body,
        grid=(DEPTH, D_MODEL // tk),
        in_specs=[
            pl.BlockSpec((COND_ROWS, tk), lambda l, k: (0, k)),
            pl.BlockSpec((None, tk, n), lambda l, k: (l, k, 0)),
            pl.BlockSpec((None, 1, n), lambda l, k: (l, 0, 0)),
        ],
        out_specs=pl.BlockSpec((None, COND_ROWS, n), lambda l, k: (l, 0, 0)),
        out_shape=jax.ShapeDtypeStruct((DEPTH, COND_ROWS, n), F32),
        compiler_params=_params("parallel", "arbitrary"),
        name="adaln",
    )(cond, w_mod, b_mod.reshape(DEPTH, 1, n))
    return out.reshape(DEPTH * COND_ROWS * N_MOD, 1, D_MODEL)


def _pick(j, values):
    out = values[-1]
    for t in range(len(values) - 2, -1, -1):
        out = jnp.where(j == t, values[t], out)
    return out


def _inproj_body(x_ref, sh_ref, sc_ref, g_ref, w_ref, p_ref, qkv_ref, h_ref, gs_ref, *, tm, slices):
    j = pl.program_id(1)
    s = _pick(j, slices)

    @pl.when(j == 0)
    def _():
        _rms_mod_rows(x_ref, h_ref, g_ref, sc_ref, sh_ref, gs_ref, tm)

    def project():
        w = w_ref[:, pl.ds(pl.multiple_of(s * BRANCH_W, BRANCH_W), BRANCH_W)]
        return jnp.dot(h_ref[...], w, preferred_element_type=F32)

    @pl.when(s < N_F32_SLICES)
    def _():
        p_ref[...] = project()

    @pl.when(s >= N_F32_SLICES)
    def _():
        qkv_ref[...] = project().astype(BF16)


def _inproj(x, mods, layer, row_of_tile, g_mix, w_in, slices=tuple(range(N_IN_SLICES))):
    rows = x.shape[0]
    tm, tn = TM_IN, BRANCH_W
    f32_blocks = [s for s in slices if s < N_F32_SLICES]
    bf16_blocks = [s - N_F32_SLICES for s in slices if s >= N_F32_SLICES]
    p_block = tuple(s if s < N_F32_SLICES else f32_blocks[-1] for s in slices)
    qkv_block = tuple(s - N_F32_SLICES if s >= N_F32_SLICES else bf16_blocks[0] for s in slices)
    return pl.pallas_call(
        functools.partial(_inproj_body, tm=tm, slices=slices),
        grid=(rows // tm, len(slices)),
        in_specs=[
            pl.BlockSpec((tm, D_MODEL), lambda i, j: (i, 0)),
            _mod_spec(layer, 0, row_of_tile(tm)),
            _mod_spec(layer, 1, row_of_tile(tm)),
            _layer_spec(layer, 1, D_MODEL),
            pl.BlockSpec((None, D_MODEL, IN_W), lambda i, j: (layer, 0, 0), pipeline_mode=pl.Buffered(1)),
        ],
        out_specs=[
            pl.BlockSpec((tm, tn), lambda i, j: (i, _pick(j, p_block))),
            pl.BlockSpec((tm, tn), lambda i, j: (i, _pick(j, qkv_block))),
            pl.BlockSpec((tm, D_MODEL), lambda i, j: (i, 0)),
        ],
        out_shape=[
            jax.ShapeDtypeStruct((rows, N_F32_SLICES * BRANCH_W), F32),
            jax.ShapeDtypeStruct((rows, (N_IN_SLICES - N_F32_SLICES) * BRANCH_W), BF16),
            jax.ShapeDtypeStruct((rows, D_MODEL), BF16),
        ],
        scratch_shapes=[pltpu.VMEM((1, D_MODEL), F32)],
        compiler_params=_params("parallel", "arbitrary"),
        name="inproj",
    )(x, mods, mods, g_mix, w_in)


def _pool_body(z_ref, w_ref, s_ref, o_ref, zp_ref, *, L):
    halo = jnp.zeros((POOL_HALO, BRANCH_W), F32)
    zp_ref[0:POOL_HALO, :] = halo
    zp_ref[POOL_HALO + L:2 * POOL_HALO + L, :] = halo
    zp_ref[POOL_HALO:POOL_HALO + L, :] = z_ref[...]
    rc = CHUNK
    for c in range(L // rc):
        r0 = c * rc
        t = r0 + lax.broadcasted_iota(jnp.int32, (rc, GROUP_W), 0)
        for gi, win in enumerate(POOL_WINDOWS):
            half = win // 2
            sl = slice(gi * GROUP_W, (gi + 1) * GROUP_W)
            acc = zp_ref[POOL_HALO + r0 - half:POOL_HALO + r0 - half + rc, sl]
            for o in range(-half + 1, half):
                acc = acc + zp_ref[POOL_HALO + r0 + o:POOL_HALO + r0 + o + rc, sl]
            cnt = (jnp.minimum(t + half, L) - jnp.maximum(t - half, 0)).astype(F32)
            pooled = acc / cnt - z_ref[r0:r0 + rc, sl]
            y = jnp.dot(pooled.astype(BF16), w_ref[gi], preferred_element_type=F32)
            o_ref[r0:r0 + rc, sl] = (y * s_ref[:, sl]).astype(BF16)


def _pool(p, L, layer, w_pool, pool_scale):
    nb = p.shape[0] // L
    return pl.pallas_call(
        functools.partial(_pool_body, L=L),
        grid=(nb,),
        in_specs=[
            pl.BlockSpec((L, BRANCH_W), lambda b: (b, S_POOL)),
            _layer_spec(layer, N_GROUPS, GROUP_W, GROUP_W),
            _layer_spec(layer, 1, BRANCH_W),
        ],
        out_specs=pl.BlockSpec((L, BRANCH_W), lambda b: (b, 0)),
        out_shape=jax.ShapeDtypeStruct((p.shape[0], BRANCH_W), BF16),
        scratch_shapes=[pltpu.VMEM((L + 2 * POOL_HALO, BRANCH_W), F32)],
        compiler_params=_params("parallel"),
        name="pool",
    )(p, w_pool, pool_scale)


def _sgu_body(u_ref, v_ref, g_ref, w_ref, b_ref, o_ref, *, L):
    for n in range(L // CHUNK):
        r0 = n * CHUNK
        v = v_ref[r0:r0 + CHUNK, :]
        vc = v - jnp.mean(v, axis=-1, keepdims=True)
        var = jnp.mean(vc * vc, axis=-1, keepdims=True)
        vn = ((vc * lax.rsqrt(var + EPS)) * g_ref[...]).astype(BF16)
        for gi in range(N_GROUPS):
            sl = slice(gi * GROUP_W, (gi + 1) * GROUP_W)
            mixed = jnp.dot(w_ref[gi], vn[:, sl], preferred_element_type=F32) + b_ref[gi]
            o_ref[r0:r0 + CHUNK, sl] = (u_ref[r0:r0 + CHUNK, sl] * mixed).astype(BF16)


def _sgu(p, L, layer, ln_g, w_sp, b_sp):
    nb = p.shape[0] // L
    return pl.pallas_call(
        functools.partial(_sgu_body, L=L),
        grid=(nb,),
        in_specs=[
            pl.BlockSpec((L, BRANCH_W), lambda b: (b, S_U)),
            pl.BlockSpec((L, BRANCH_W), lambda b: (b, S_V)),
            _layer_spec(layer, 1, BRANCH_W),
            _layer_spec(layer, N_GROUPS, CHUNK, CHUNK),
            _layer_spec(layer, N_GROUPS, CHUNK, 1),
        ],
        out_specs=pl.BlockSpec((L, BRANCH_W), lambda b: (b, 0)),
        out_shape=jax.ShapeDtypeStruct((p.shape[0], BRANCH_W), BF16),
        compiler_params=_params("parallel"),
        name="sgu",
    )(p, p, ln_g, w_sp, b_sp)


def _lru_scan(a_ref, b_ref, h0, nblk, reverse, emit):
    row = lax.broadcasted_iota(jnp.int32, (SUBLANES, BRANCH_W), 0)

    def body(jj, h):
        j = (nblk - 1 - jj) if reverse else jj
        r = pl.multiple_of(j * SUBLANES, SUBLANES)
        a = a_ref[pl.ds(r, SUBLANES), :]
        b = b_ref[pl.ds(r, SUBLANES), :]
        for s in (1, 2, 4):
            shift = (SUBLANES - s) if reverse else s
            a_s = pltpu.roll(a, shift, 0)
            b_s = pltpu.roll(b, shift, 0)
            m = (row < SUBLANES - s) if reverse else (row >= s)
            b = jnp.where(m, a * b_s + b, b)
            a = jnp.where(m, a * a_s, a)
        hh = a * h + b
        emit(r, hh)
        edge = hh[0:1, :] if reverse else hh[SUBLANES - 1:SUBLANES, :]
        return jnp.broadcast_to(edge, (SUBLANES, BRANCH_W))

    return lax.fori_loop(0, nblk, body, h0, unroll=4)


def _lru_body(z_ref, lg_ref, cw_ref, cb_ref, wr_ref, br_ref, wi_ref, bi_ref, lam_ref, h0_ref,
              y_ref, hl_ref, zp_ref, a_ref, b_ref, hs_ref, *, L):
    halo = jnp.zeros((CONV_HALO, BRANCH_W), F32)
    zp_ref[0:CONV_HALO, :] = halo
    zp_ref[CONV_HALO + L:2 * CONV_HALO + L, :] = halo
    zp_ref[CONV_HALO:CONV_HALO + L, :] = z_ref[...]
    left = CONV_W // 2
    rc = CHUNK
    nblk = L // SUBLANES

    def conv_chunk(r0):
        out = zp_ref[CONV_HALO + r0 - left:CONV_HALO + r0 - left + rc, :] * cw_ref[0:1, :] + cb_ref[...]
        for k in range(1, CONV_W):
            lo = CONV_HALO + r0 - left + k
            out = out + zp_ref[lo:lo + rc, :] * cw_ref[k:k + 1, :]
        return out

    def sigmoid(t):
        return 0.5 * jnp.tanh(0.5 * t) + 0.5

    sp = [jax.nn.softplus(-lam_ref[d]) for d in range(2)]
    for c in range(L // rc):
        r0 = c * rc
        cv = conv_chunk(r0)
        cvb = cv.astype(BF16)
        for d in range(2):
            for gi in range(N_GROUPS):
                sl = slice(gi * GROUP_W, (gi + 1) * GROUP_W)
                rg = sigmoid(jnp.dot(cvb[:, sl], wr_ref[d, gi], preferred_element_type=F32) + br_ref[d][:, sl])
                ig = sigmoid(jnp.dot(cvb[:, sl], wi_ref[d, gi], preferred_element_type=F32) + bi_ref[d][:, sl])
                log_a = (-LRU_C * rg) * sp[d][:, sl]
                a = jnp.exp(log_a)
                a_ref[d, r0:r0 + rc, sl] = a
                one_minus_a2 = -jnp.tanh(log_a) * (a * a + 1.0)
                b_ref[d, r0:r0 + rc, sl] = jnp.sqrt(one_minus_a2) * (ig * cv[:, sl])

    for d in range(2):
        reverse = d == 1
        h0 = jnp.broadcast_to(h0_ref[0, d:d + 1, :], (SUBLANES, BRANCH_W))
        if not reverse:
            def emit(r, hh):
                hs_ref[pl.ds(r, SUBLANES), :] = hh
        else:
            def emit(r, hh):
                tot = hs_ref[pl.ds(r, SUBLANES), :] + hh
                y_ref[pl.ds(r, SUBLANES), :] = (tot * jax.nn.gelu(lg_ref[pl.ds(r, SUBLANES), :])).astype(BF16)
        h_end = _lru_scan(a_ref.at[d], b_ref.at[d], h0, nblk, reverse, emit)
        hl_ref[0, d:d + 1, :] = h_end[0:1, :]


def _lru(p, L, layer, conv_w, conv_b, w_rg, b_rg, w_ig, b_ig, lam, h0):
    nb = p.shape[0] // L
    gate_w = _layer_spec(layer, 2, N_GROUPS, GROUP_W, GROUP_W)
    dir_vec = _layer_spec(layer, 2, 1, BRANCH_W)
    return pl.pallas_call(
        functools.partial(_lru_body, L=L),
        grid=(nb,),
        in_specs=[
            pl.BlockSpec((L, BRANCH_W), lambda b: (b, S_LRU)),
            pl.BlockSpec((L, BRANCH_W), lambda b: (b, S_LG)),
            _layer_spec(layer, CONV_W, BRANCH_W),
            _layer_spec(layer, 1, BRANCH_W),
            gate_w, dir_vec, gate_w, dir_vec, dir_vec,
            pl.BlockSpec((1, 2, BRANCH_W), lambda b: (b, 0, 0)),
        ],
        out_specs=[
            pl.BlockSpec((L, BRANCH_W), lambda b: (b, 0)),
            pl.BlockSpec((1, 2, BRANCH_W), lambda b: (b, 0, 0)),
        ],
        out_shape=[
            jax.ShapeDtypeStruct((p.shape[0], BRANCH_W), BF16),
            jax.ShapeDtypeStruct((nb, 2, BRANCH_W), F32),
        ],
        scratch_shapes=[
            pltpu.VMEM((L + 2 * CONV_HALO, BRANCH_W), F32),
            pltpu.VMEM((2, L, BRANCH_W), F32),
            pltpu.VMEM((2, L, BRANCH_W), F32),
            pltpu.VMEM((L, BRANCH_W), F32),
        ],
        compiler_params=_params("parallel"),
        name="lru",
    )(p, p, conv_w, conv_b, w_rg, b_rg, w_ig, b_ig, lam, h0)


_NT = (((1,), (1,)), ((), ()))


def _softmax_pv(scores, values):
    m = scores[0].max(axis=-1, keepdims=True)
    for s in scores[1:]:
        m = jnp.maximum(m, s.max(axis=-1, keepdims=True))
    es = [jnp.exp2(s - m) for s in scores]
    tot = es[0].sum(axis=-1, keepdims=True)
    for e in es[1:]:
        tot = tot + e.sum(axis=-1, keepdims=True)
    out = None
    for e, v in zip(es, values):
        o = jnp.dot(e.astype(BF16), v, preferred_element_type=F32)
        out = o if out is None else out + o
    return out * (1.0 / tot)


def _nattn_body(q_ref, k_ref, v_ref, kc_ref, vc_ref, t_ref, o_ref):
    rows = SEQ // GRID_W
    nq = NA_GROUP * GRID_W
    nk = NA_KEY_ROWS * GRID_W
    scale = GROUP_W ** -0.5 * LOG2_E
    kc = kc_ref[...]
    vc = vc_ref[...]

    def in_window(kr, row0):
        return ((kr >= row0) & (kr < row0 + NA_WIN_R)).astype(jnp.int32)

    def body(gi, carry):
        r0 = gi * NA_GROUP
        base = jnp.clip(r0 - NA_WIN_R // 2, 0, rows - NA_KEY_ROWS)
        qs = pl.multiple_of(r0 * GRID_W, nq)
        ks = pl.multiple_of(base * GRID_W, GRID_W)
        q = q_ref[pl.ds(qs, nq), :]
        kw = k_ref[pl.ds(ks, nk), :]
        vw = v_ref[pl.ds(ks, nk), :]
        s = lax.dot_general(q, kw, _NT, preferred_element_type=F32) * scale
        bias_rows = []
        for g in range(NA_GROUP):
            r = r0 + g
            row0 = jnp.clip(r - NA_WIN_R // 2, 0, rows - NA_WIN_R)
            pieces = []
            for jj in range(NA_KEY_ROWS // 2):
                kr = base + 2 * jj
                variant = in_window(kr, row0) + 2 * in_window(kr + 1, row0)
                e = jnp.clip(kr - r + NA_WIN_R, 0, NA_D - 1)
                pieces.append(t_ref[variant * NA_D + e])
            bias_rows.append(jnp.concatenate(pieces, axis=1))
        bias = jnp.concatenate(bias_rows, axis=0)
        s = jnp.where(bias > 0.5 * NEG_INF, s + bias, NEG_INF)
        sc = lax.dot_general(q, kc, _NT, preferred_element_type=F32) * scale
        o_ref[pl.ds(qs, nq), :] = _softmax_pv([s, sc], [vw, vc]).astype(BF16)
        return carry

    lax.fori_loop(0, rows // NA_GROUP, body, 0, unroll=2)


def _nattn(qkv, qkv_c, layer, bias_table):
    return pl.pallas_call(
        _nattn_body,
        grid=(BATCH, N_GROUPS),
        in_specs=[
            pl.BlockSpec((SEQ, GROUP_W), lambda b, h: (b, Q_COL + h)),
            pl.BlockSpec((SEQ, GROUP_W), lambda b, h: (b, K_COL + h)),
            pl.BlockSpec((SEQ, GROUP_W), lambda b, h: (b, V_COL + h)),
            pl.BlockSpec((CTX_LEN, GROUP_W), lambda b, h: (b, K_COL + h)),
            pl.BlockSpec((CTX_LEN, GROUP_W), lambda b, h: (b, V_COL + h)),
            pl.BlockSpec((None, None, 4 * NA_D, GRID_W, 2 * GRID_W), lambda b, h: (layer, h, 0, 0, 0)),
        ],
        out_specs=pl.BlockSpec((SEQ, GROUP_W), lambda b, h: (b, h)),
        out_shape=jax.ShapeDtypeStruct((qkv.shape[0], BRANCH_W), BF16),
        compiler_params=_params("parallel", "parallel"),
        name="nattn",
    )(qkv, qkv, qkv, qkv_c, qkv_c, bias_table)


def _cattn_body(q_ref, k_ref, v_ref, o_ref):
    scale = GROUP_W ** -0.5 * LOG2_E
    s = lax.dot_general(q_ref[...], k_ref[...], _NT, preferred_element_type=F32) * scale
    o_ref[...] = _softmax_pv([s], [v_ref[...]]).astype(BF16)


def _cattn(qkv_c):
    return pl.pallas_call(
        _cattn_body,
        grid=(BATCH, N_GROUPS),
        in_specs=[
            pl.BlockSpec((CTX_LEN, GROUP_W), lambda b, h: (b, Q_COL + h)),
            pl.BlockSpec((CTX_LEN, GROUP_W), lambda b, h: (b, K_COL + h)),
            pl.BlockSpec((CTX_LEN, GROUP_W), lambda b, h: (b, V_COL + h)),
        ],
        out_specs=pl.BlockSpec((CTX_LEN, GROUP_W), lambda b, h: (b, h)),
        out_shape=jax.ShapeDtypeStruct((qkv_c.shape[0], BRANCH_W), BF16),
        compiler_params=_params("parallel", "parallel"),
        name="cattn",
    )(qkv_c, qkv_c, qkv_c)


def _bias_table(rpb):
    col = jnp.arange(GRID_W)
    col_start = jnp.clip(col - NA_WIN_C // 2, 0, GRID_W - NA_WIN_C)
    col_ok = (col[None, :] >= col_start[:, None]) & (col[None, :] < col_start[:, None] + NA_WIN_C)
    pad = GRID_W - NA_WIN_C
    period = 2 * GRID_W - 1
    u = jnp.pad(rpb.astype(F32), [(0, 0)] * 3 + [(pad, pad)])
    skew = jnp.tile(u, GRID_W + 1)[..., :GRID_W * (period + 1)]
    skew = skew.reshape(rpb.shape[:3] + (GRID_W, period + 1))
    toe = skew[..., ::-1, :GRID_W]
    tab = jnp.where(col_ok, toe * LOG2_E, NEG_INF)
    off = jnp.full_like(tab[:, :, :1], NEG_INF)
    lo = jnp.concatenate([off, tab], axis=2)
    hi = jnp.concatenate([tab, off], axis=2)
    none = jnp.full_like(lo, NEG_INF)
    variants = [jnp.concatenate([lo if v & 1 else none, hi if v & 2 else none], axis=-1) for v in range(4)]
    return jnp.concatenate(variants, axis=2)


def _merge_body(h_ref, y0_ref, y1_ref, y2_ref, y3_ref, g0_ref, g1_ref, g2_ref, g3_ref, wb_ref, o_ref):
    h = h_ref[...]
    acc = None
    for n, (y_ref, g_ref) in enumerate(((y0_ref, g0_ref), (y1_ref, g1_ref), (y2_ref, g2_ref), (y3_ref, g3_ref))):
        gate = jax.nn.sigmoid(jnp.dot(h, g_ref[...], preferred_element_type=F32))
        term = gate * jnp.dot(y_ref[...], wb_ref[n], preferred_element_type=F32)
        acc = term if acc is None else acc + term
    o_ref[...] = acc.astype(BF16)


def _merge(h, ys, layer, w_gate, w_branch):
    rows = h.shape[0]
    tm, tn = TM_MERGE, 512
    nj = D_MODEL // tn
    gate_spec = lambda n: pl.BlockSpec((D_MODEL, tn), lambda i, j: (0, n * nj + j))
    y_spec = pl.BlockSpec((tm, BRANCH_W), lambda i, j: (i, 0))
    return pl.pallas_call(
        _merge_body,
        grid=(rows // tm, nj),
        in_specs=[pl.BlockSpec((tm, D_MODEL), lambda i, j: (i, 0)), y_spec, y_spec, y_spec, y_spec,
                  gate_spec(0), gate_spec(1), gate_spec(2), gate_spec(3),
                  pl.BlockSpec((None, N_BRANCH, BRANCH_W, tn), lambda i, j: (layer, 0, 0, j))],
        out_specs=pl.BlockSpec((tm, tn), lambda i, j: (i, j)),
        out_shape=jax.ShapeDtypeStruct((rows, D_MODEL), BF16),
        compiler_params=_params("parallel", "arbitrary"),
        name="merge",
    )(h, *ys, w_gate, w_gate, w_gate, w_gate, w_branch)


def _outproj_body(m_ref, x_ref, w_ref, gt_ref, g_ref, sh_ref, sc_ref, xm_ref, h2_ref, gs_ref, *, tm):
    xm_ref[...] = x_ref[...] + gt_ref[...] * jnp.dot(m_ref[...], w_ref[...], preferred_element_type=F32)
    _rms_mod_rows(xm_ref, h2_ref, g_ref, sc_ref, sh_ref, gs_ref, tm)


def _outproj(merged, x, mods, layer, row_of_tile, w_out, g_ffn):
    rows = x.shape[0]
    tm = TM_OUT
    row_spec = pl.BlockSpec((tm, D_MODEL), lambda i: (i, 0))
    return pl.pallas_call(
        functools.partial(_outproj_body, tm=tm),
        grid=(rows // tm,),
        in_specs=[row_spec, row_spec,
                  _layer_spec(layer, D_MODEL, D_MODEL),
                  _mod_spec(layer, 2, row_of_tile(tm)),
                  _layer_spec(layer, 1, D_MODEL),
                  _mod_spec(layer, 3, row_of_tile(tm)),
                  _mod_spec(layer, 4, row_of_tile(tm))],
        out_specs=[row_spec, row_spec],
        out_shape=[jax.ShapeDtypeStruct((rows, D_MODEL), F32),
                   jax.ShapeDtypeStruct((rows, D_MODEL), BF16)],
        scratch_shapes=[pltpu.VMEM((1, D_MODEL), F32)],
        compiler_params=_params("parallel"),
        name="outproj",
    )(merged, x, w_out, mods, g_ffn, mods, mods)


def _ffn_body(h_ref, wg_ref, wu_ref, wd_ref, xm_ref, gt_ref, gf_ref, o_ref, a_ref, rs_ref, *, tm, tf, final):
    f = pl.program_id(1)

    h = h_ref[...]
    for s in range(tf // FFN_SUB):
        sl = slice(s * FFN_SUB, (s + 1) * FFN_SUB)
        g = jnp.dot(h, wg_ref[:, sl], preferred_element_type=F32)
        u = jnp.dot(h, wu_ref[:, sl], preferred_element_type=F32)
        a_ref[:, sl] = (jax.nn.silu(g) * u).astype(BF16)

    @pl.when(f == 0)
    def _():
        o_ref[...] = jnp.dot(a_ref[...], wd_ref[...], preferred_element_type=F32)

    @pl.when(f > 0)
    def _():
        o_ref[...] += jnp.dot(a_ref[...], wd_ref[...], preferred_element_type=F32)

    @pl.when(f == pl.num_programs(1) - 1)
    def _():
        def residual(c, carry):
            r = pl.multiple_of(c * NORM_ROWS, NORM_ROWS)
            y = xm_ref[pl.ds(r, NORM_ROWS), :] + gt_ref[...] * o_ref[pl.ds(r, NORM_ROWS), :]
            o_ref[pl.ds(r, NORM_ROWS), :] = y
            if final:
                ms = jnp.mean(y * y, axis=-1, keepdims=True)
                rs_ref[pl.ds(r, NORM_ROWS), :] = jnp.broadcast_to(lax.rsqrt(ms + EPS), (NORM_ROWS, LANES))
            return carry
        lax.fori_loop(0, tm // NORM_ROWS, residual, 0, unroll=4)

        if final:
            def scale(c, carry):
                r = pl.multiple_of(c * NORM_ROWS, NORM_ROWS)
                rs = jnp.concatenate([rs_ref[pl.ds(r, NORM_ROWS), :]] * (D_MODEL // LANES), axis=1)
                o_ref[pl.ds(r, NORM_ROWS), :] = (o_ref[pl.ds(r, NORM_ROWS), :] * rs) * gf_ref[...]
                return carry
            lax.fori_loop(0, tm // NORM_ROWS, scale, 0, unroll=4)


def _ffn(h2, xm, mods, layer, row_of_tile, w_g, w_u, w_d, g_final, final):
    rows = xm.shape[0]
    tm, tf = TM_FFN, 512
    row_spec = pl.BlockSpec((tm, D_MODEL), lambda i, f: (i, 0))
    return pl.pallas_call(
        functools.partial(_ffn_body, tm=tm, tf=tf, final=final),
        grid=(rows // tm, D_FF // tf),
        in_specs=[row_spec,
                  pl.BlockSpec((None, D_MODEL, tf), lambda i, f: (layer, 0, f)),
                  pl.BlockSpec((None, D_MODEL, tf), lambda i, f: (layer, 0, f)),
                  pl.BlockSpec((None, tf, D_MODEL), lambda i, f: (layer, f, 0)),
                  row_spec,
                  _mod_spec(layer, 5, row_of_tile(tm)),
                  pl.BlockSpec((1, D_MODEL), lambda i, f: (0, 0))],
        out_specs=row_spec,
        out_shape=jax.ShapeDtypeStruct((rows, D_MODEL), F32),
        scratch_shapes=[pltpu.VMEM((tm, tf), BF16), pltpu.VMEM((tm, LANES), F32)],
        compiler_params=_params("parallel", "arbitrary"),
        name="ffn",
    )(h2, w_g, w_u, w_d, xm, mods, g_final)


def kernel(x, c, ctx, c_ctx, w_mod, b_mod, g_mix, g_ffn, g_final, w_in, w_pool, pool_scale, gmlp_ln_g, w_sp, b_sp, conv_w, conv_b, w_rg, b_rg, w_ig, b_ig, lru_lam, rpb, w_branch, w_gate, w_out, w_ffn_gate, w_ffn_up, w_ffn_down):
    B, L, D = x.shape
    C = ctx.shape[1]
    assert (B, L, D, C) == (BATCH, SEQ, D_MODEL, CTX_LEN)
    xs = x.reshape(B * L, D)
    cs = ctx.reshape(B * C, D)

    cond = jnp.concatenate([c, c_ctx[None, :], jnp.zeros((COND_ROWS - B - 1, D), F32)], axis=0)
    mods = _adaln_all(cond, w_mod, b_mod)

    w_in_b = w_in.astype(BF16)
    w_gate_b = [w_gate[l].astype(BF16).reshape(D, N_BRANCH * D) for l in range(DEPTH)]
    w_branch_b = w_branch.astype(BF16)
    w_out_b = w_out.astype(BF16)
    w_fg, w_fu, w_fd = w_ffn_gate.astype(BF16), w_ffn_up.astype(BF16), w_ffn_down.astype(BF16)
    g_mix_r = g_mix.reshape(DEPTH, 1, D)
    g_ffn_r = g_ffn.reshape(DEPTH, 1, D)
    g_fin_r = g_final.reshape(1, D)
    pool_args = (w_pool.astype(BF16), pool_scale.reshape(DEPTH, 1, BRANCH_W))
    sgu_args = (gmlp_ln_g.reshape(DEPTH, 1, BRANCH_W), w_sp.astype(BF16), b_sp.reshape(DEPTH, N_GROUPS, CHUNK, 1))
    lru_args = (conv_w, conv_b.reshape(DEPTH, 1, BRANCH_W),
                w_rg.astype(BF16), b_rg.reshape(DEPTH, 2, 1, BRANCH_W),
                w_ig.astype(BF16), b_ig.reshape(DEPTH, 2, 1, BRANCH_W),
                lru_lam.reshape(DEPTH, 2, 1, BRANCH_W))
    bias_table = _bias_table(rpb)
    h_zero = jnp.zeros((B, 2, BRANCH_W), F32)

    for l in range(DEPTH):
        last = l == DEPTH - 1
        px, qkv, hx = _inproj(xs, mods, l, _latent_row, g_mix_r, w_in_b)
        ctx_slices = (S_LRU, S_LG, S_K, S_VAL) if last else tuple(range(N_IN_SLICES))
        pc, qkv_c, hc = _inproj(cs, mods, l, _context_row, g_mix_r, w_in_b, ctx_slices)

        yc_lru, hc_state = _lru(pc, C, l, *lru_args, h_zero)

        y_pool = _pool(px, L, l, *pool_args)
        y_sgu = _sgu(px, L, l, *sgu_args)
        y_lru, _ = _lru(px, L, l, *lru_args, hc_state)
        y_na = _nattn(qkv, qkv_c, l, bias_table)
        merged = _merge(hx, (y_pool, y_sgu, y_lru, y_na), l, w_gate_b[l], w_branch_b)
        xm, hx2 = _outproj(merged, xs, mods, l, _latent_row, w_out_b, g_ffn_r)

        if not last:
            yc_pool = _pool(pc, C, l, *pool_args)
            yc_sgu = _sgu(pc, C, l, *sgu_args)
            yc_na = _cattn(qkv_c)
            merged_c = _merge(hc, (yc_pool, yc_sgu, yc_lru, yc_na), l, w_gate_b[l], w_branch_b)
            cm, hc2 = _outproj(merged_c, cs, mods, l, _context_row, w_out_b, g_ffn_r)
            cs = _ffn(hc2, cm, mods, l, _context_row, w_fg, w_fu, w_fd, g_fin_r, False)

        xs = _ffn(hx2, xm, mods, l, _latent_row, w_fg, w_fu, w_fd, g_fin_r, last)

    return xs.reshape(B, L, D)
```

```python
import functools

import jax
import jax.numpy as jnp
from jax import lax
from jax.experimental import pallas as pl
from jax.experimental.pallas import tpu as pltpu

D_MODEL = 2048
BATCH = 8
SEQ = 2048
DEPTH = 2
CTX_LEN = 256
GRID_W = 64
N_BRANCH = 4
BRANCH_W = D_MODEL // N_BRANCH
GROUP_W = 128
N_GROUPS = BRANCH_W // GROUP_W
N_IN_SLICES = 8
IN_W = N_IN_SLICES * BRANCH_W
POOL_WINDOWS = (2, 4, 8, 16)
CHUNK = 128
CONV_W = 4
LRU_C = 8.0
NA_WIN_R = 8
NA_WIN_C = 16
D_FF = 5632
EPS = 1e-6
NEG_INF = -1e30
LOG2_E = 1.4426950408889634

N_MOD = 6
COND_ROWS = 16
CTX_COND_ROW = BATCH
SUBLANES = 8
LANES = 128
POOL_HALO = 16
CONV_HALO = 8
VMEM_LIMIT = 60 * 1024 * 1024

F32 = jnp.float32
BF16 = jnp.bfloat16

S_POOL, S_U, S_V, S_LRU, S_LG, S_Q, S_K, S_VAL = range(N_IN_SLICES)
N_F32_SLICES = S_Q
HEADS_PER_SLICE = BRANCH_W // GROUP_W
Q_COL, K_COL, V_COL = (HEADS_PER_SLICE * (s - N_F32_SLICES) for s in (S_Q, S_K, S_VAL))
NA_GROUP = 4
NA_KEY_ROWS = 12
NA_D = 2 * NA_WIN_R
FFN_SUB = 256
NORM_ROWS = 16
NORM_GROUP = 8

TM_IN, TM_MERGE, TM_OUT, TM_FFN = 1024, 1024, 512, 1024


def _params(*sem):
    return pltpu.CompilerParams(dimension_semantics=sem, vmem_limit_bytes=VMEM_LIMIT)


def _layer_spec(layer, *shape):
    zeros = (0,) * len(shape)
    return pl.BlockSpec((None,) + shape, lambda *_: (layer,) + zeros)


def _mod_spec(layer, which, row_of_tile):
    def index(i, *_):
        return ((layer * COND_ROWS + row_of_tile(i)) * N_MOD + which, 0, 0)
    return pl.BlockSpec((None, 1, D_MODEL), index)


def _latent_row(tm):
    return lambda i: i // (SEQ // tm)


def _context_row(tm):
    return lambda i: CTX_COND_ROW


def _rms_mod_rows(src_ref, dst_ref, g_ref, sc_ref, sh_ref, gs_ref, rows):
    gs_ref[...] = g_ref[...] * (1.0 + sc_ref[...])
    group = NORM_GROUP * NORM_ROWS

    def body(c, carry):
        base = pl.multiple_of(c * group, group)
        scales = []
        for k in range(NORM_GROUP):
            x = src_ref[pl.ds(base + k * NORM_ROWS, NORM_ROWS), :]
            scales.append(lax.rsqrt(jnp.mean(x * x, axis=-1, keepdims=True) + EPS))
        for k in range(NORM_GROUP):
            x = src_ref[pl.ds(base + k * NORM_ROWS, NORM_ROWS), :]
            y = (x * scales[k]) * gs_ref[...] + sh_ref[...]
            dst_ref[pl.ds(base + k * NORM_ROWS, NORM_ROWS), :] = y.astype(dst_ref.dtype)
        return carry
    lax.fori_loop(0, rows // group, body, 0)


def _mod_body(c_ref, w_ref, b_ref, o_ref):
    @pl.when(pl.program_id(1) == 0)
    def _():
        o_ref[...] = jnp.broadcast_to(b_ref[...], o_ref.shape)
    a = jax.nn.silu(c_ref[...]).astype(BF16)
    o_ref[...] += jnp.dot(a, w_ref[...].astype(BF16), preferred_element_type=F32)


def _adaln_all(cond, w_mod, b_mod):
    tk = 256
    n = N_MOD * D_MODEL
    out = pl.pallas_call(
        _mod_body,
        grid=(DEPTH, D_MODEL // tk),
        in_specs=[
            pl.BlockSpec((COND_ROWS, tk), lambda l, k: (0, k)),
            pl.BlockSpec((None, tk, n), lambda l, k: (l, k, 0)),
            pl.BlockSpec((None, 1, n), lambda l, k: (l, 0, 0)),
        ],
        out_specs=pl.BlockSpec((None, COND_ROWS, n), lambda l, k: (l, 0, 0)),
        out_shape=jax.ShapeDtypeStruct((DEPTH, COND_ROWS, n), F32),
        compiler_params=_params("parallel", "arbitrary"),
        name="adaln",
    )(cond, w_mod, b_mod.reshape(DEPTH, 1, n))
    return out.reshape(DEPTH * COND_ROWS * N_MOD, 1, D_MODEL)


def _pick(j, values):
    out = values[-1]
    for t in range(len(values) - 2, -1, -1):
        out = jnp.where(j == t, values[t], out)
    return out


def _inproj_body(x_ref, sh_ref, sc_ref, g_ref, w_ref, p_ref, qkv_ref, h_ref, gs_ref, *, tm, slices):
    j = pl.program_id(1)
    s = _pick(j, slices)

    @pl.when(j == 0)
    def _():
        _rms_mod_rows(x_ref, h_ref, g_ref, sc_ref, sh_ref, gs_ref, tm)

    def project():
        w = w_ref[:, pl.ds(pl.multiple_of(s * BRANCH_W, BRANCH_W), BRANCH_W)]
        return jnp.dot(h_ref[...], w, preferred_element_type=F32)

    @pl.when(s < N_F32_SLICES)
    def _():
        p_ref[...] = project()

    @pl.when(s >= N_F32_SLICES)
    def _():
        qkv_ref[...] = project().astype(BF16)


def _inproj(x, mods, layer, row_of_tile, g_mix, w_in, slices=tuple(range(N_IN_SLICES))):
    rows = x.shape[0]
    tm, tn = TM_IN, BRANCH_W
    f32_blocks = [s for s in slices if s < N_F32_SLICES]
    bf16_blocks = [s - N_F32_SLICES for s in slices if s >= N_F32_SLICES]
    p_block = tuple(s if s < N_F32_SLICES else f32_blocks[-1] for s in slices)
    qkv_block = tuple(s - N_F32_SLICES if s >= N_F32_SLICES else bf16_blocks[0] for s in slices)
    return pl.pallas_call(
        functools.partial(_inproj_body, tm=tm, slices=slices),
        grid=(rows // tm, len(slices)),
        in_specs=[
            pl.BlockSpec((tm, D_MODEL), lambda i, j: (i, 0)),
            _mod_spec(layer, 0, row_of_tile(tm)),
            _mod_spec(layer, 1, row_of_tile(tm)),
            _layer_spec(layer, 1, D_MODEL),
            pl.BlockSpec((None, D_MODEL, IN_W), lambda i, j: (layer, 0, 0), pipeline_mode=pl.Buffered(1)),
        ],
        out_specs=[
            pl.BlockSpec((tm, tn), lambda i, j: (i, _pick(j, p_block))),
            pl.BlockSpec((tm, tn), lambda i, j: (i, _pick(j, qkv_block))),
            pl.BlockSpec((tm, D_MODEL), lambda i, j: (i, 0)),
        ],
        out_shape=[
            jax.ShapeDtypeStruct((rows, N_F32_SLICES * BRANCH_W), F32),
            jax.ShapeDtypeStruct((rows, (N_IN_SLICES - N_F32_SLICES) * BRANCH_W), BF16),
            jax.ShapeDtypeStruct((rows, D_MODEL), BF16),
        ],
        scratch_shapes=[pltpu.VMEM((1, D_MODEL), F32)],
        compiler_params=_params("parallel", "arbitrary"),
        name="inproj",
    )(x, mods, mods, g_mix, w_in)


def _pool_body(z_ref, w_ref, s_ref, o_ref, zp_ref, *, L):
    halo = jnp.zeros((POOL_HALO, BRANCH_W), F32)
    zp_ref[0:POOL_HALO, :] = halo
    zp_ref[POOL_HALO + L:2 * POOL_HALO + L, :] = halo
    zp_ref[POOL_HALO:POOL_HALO + L, :] = z_ref[...]
    rc = CHUNK
    for c in range(L // rc):
        r0 = c * rc
        t = r0 + lax.broadcasted_iota(jnp.int32, (rc, GROUP_W), 0)
        for gi, win in enumerate(POOL_WINDOWS):
            half = win // 2
            sl = slice(gi * GROUP_W, (gi + 1) * GROUP_W)
            acc = zp_ref[POOL_HALO + r0 - half:POOL_HALO + r0 - half + rc, sl]
            for o in range(-half + 1, half):
                acc = acc + zp_ref[POOL_HALO + r0 + o:POOL_HALO + r0 + o + rc, sl]
            cnt = (jnp.minimum(t + half, L) - jnp.maximum(t - half, 0)).astype(F32)
            pooled = acc / cnt - z_ref[r0:r0 + rc, sl]
            y = jnp.dot(pooled.astype(BF16), w_ref[gi], preferred_element_type=F32)
            o_ref[r0:r0 + rc, sl] = (y * s_ref[:, sl]).astype(BF16)


def _pool(p, L, layer, w_pool, pool_scale):
    nb = p.shape[0] // L
    return pl.pallas_call(
        functools.partial(_pool_body, L=L),
        grid=(nb,),
        in_specs=[
            pl.BlockSpec((L, BRANCH_W), lambda b: (b, S_POOL)),
            _layer_spec(layer, N_GROUPS, GROUP_W, GROUP_W),
            _layer_spec(layer, 1, BRANCH_W),
        ],
        out_specs=pl.BlockSpec((L, BRANCH_W), lambda b: (b, 0)),
        out_shape=jax.ShapeDtypeStruct((p.shape[0], BRANCH_W), BF16),
        scratch_shapes=[pltpu.VMEM((L + 2 * POOL_HALO, BRANCH_W), F32)],
        compiler_params=_params("parallel"),
        name="pool",
    )(p, w_pool, pool_scale)


def _sgu_body(u_ref, v_ref, g_ref, w_ref, b_ref, o_ref, *, L):
    for n in range(L // CHUNK):
        r0 = n * CHUNK
        v = v_ref[r0:r0 + CHUNK, :]
        vc = v - jnp.mean(v, axis=-1, keepdims=True)
        var = jnp.mean(vc * vc, axis=-1, keepdims=True)
        vn = ((vc * lax.rsqrt(var + EPS)) * g_ref[...]).astype(BF16)
        for gi in range(N_GROUPS):
            sl = slice(gi * GROUP_W, (gi + 1) * GROUP_W)
            mixed = jnp.dot(w_ref[gi], vn[:, sl], preferred_element_type=F32) + b_ref[gi]
            o_ref[r0:r0 + CHUNK, sl] = (u_ref[r0:r0 + CHUNK, sl] * mixed).astype(BF16)


def _sgu(p, L, layer, ln_g, w_sp, b_sp):
    nb = p.shape[0] // L
    return pl.pallas_call(
        functools.partial(_sgu_body, L=L),
        grid=(nb,),
        in_specs=[
            pl.BlockSpec((L, BRANCH_W), lambda b: (b, S_U)),
            pl.BlockSpec((L, BRANCH_W), lambda b: (b, S_V)),
            _layer_spec(layer, 1, BRANCH_W),
            _layer_spec(layer, N_GROUPS, CHUNK, CHUNK),
            _layer_spec(layer, N_GROUPS, CHUNK, 1),
        ],
        out_specs=pl.BlockSpec((L, BRANCH_W), lambda b: (b, 0)),
        out_shape=jax.ShapeDtypeStruct((p.shape[0], BRANCH_W), BF16),
        compiler_params=_params("parallel"),
        name="sgu",
    )(p, p, ln_g, w_sp, b_sp)


def _lru_scan(a_ref, b_ref, h0, nblk, reverse, emit):
    row = lax.broadcasted_iota(jnp.int32, (SUBLANES, BRANCH_W), 0)

    def body(jj, h):
        j = (nblk - 1 - jj) if reverse else jj
        r = pl.multiple_of(j * SUBLANES, SUBLANES)
        a = a_ref[pl.ds(r, SUBLANES), :]
        b = b_ref[pl.ds(r, SUBLANES), :]
        for s in (1, 2, 4):
            shift = (SUBLANES - s) if reverse else s
            a_s = pltpu.roll(a, shift, 0)
            b_s = pltpu.roll(b, shift, 0)
            m = (row < SUBLANES - s) if reverse else (row >= s)
            b = jnp.where(m, a * b_s + b, b)
            a = jnp.where(m, a * a_s, a)
        hh = a * h + b
        emit(r, hh)
        edge = hh[0:1, :] if reverse else hh[SUBLANES - 1:SUBLANES, :]
        return jnp.broadcast_to(edge, (SUBLANES, BRANCH_W))

    return lax.fori_loop(0, nblk, body, h0, unroll=4)


def _lru_body(z_ref, lg_ref, cw_ref, cb_ref, wr_ref, br_ref, wi_ref, bi_ref, lam_ref, h0_ref,
              y_ref, hl_ref, zp_ref, a_ref, b_ref, hs_ref, *, L):
    halo = jnp.zeros((CONV_HALO, BRANCH_W), F32)
    zp_ref[0:CONV_HALO, :] = halo
    zp_ref[CONV_HALO + L:2 * CONV_HALO + L, :] = halo
    zp_ref[CONV_HALO:CONV_HALO + L, :] = z_ref[...]
    left = CONV_W // 2
    rc = CHUNK
    nblk = L // SUBLANES

    def conv_chunk(r0):
        out = zp_ref[CONV_HALO + r0 - left:CONV_HALO + r0 - left + rc, :] * cw_ref[0:1, :] + cb_ref[...]
        for k in range(1, CONV_W):
            lo = CONV_HALO + r0 - left + k
            out = out + zp_ref[lo:lo + rc, :] * cw_ref[k:k + 1, :]
        return out

    def sigmoid(t):
        return 0.5 * jnp.tanh(0.5 * t) + 0.5

    sp = [jax.nn.softplus(-lam_ref[d]) for d in range(2)]
    for c in range(L // rc):
        r0 = c * rc
        cv = conv_chunk(r0)
        cvb = cv.astype(BF16)
        for d in range(2):
            for gi in range(N_GROUPS):
                sl = slice(gi * GROUP_W, (gi + 1) * GROUP_W)
                rg = sigmoid(jnp.dot(cvb[:, sl], wr_ref[d, gi], preferred_element_type=F32) + br_ref[d][:, sl])
                ig = sigmoid(jnp.dot(cvb[:, sl], wi_ref[d, gi], preferred_element_type=F32) + bi_ref[d][:, sl])
                log_a = (-LRU_C * rg) * sp[d][:, sl]
                a = jnp.exp(log_a)
                a_ref[d, r0:r0 + rc, sl] = a
                one_minus_a2 = -jnp.tanh(log_a) * (a * a + 1.0)
                b_ref[d, r0:r0 + rc, sl] = jnp.sqrt(one_minus_a2) * (ig * cv[:, sl])

    for d in range(2):
        reverse = d == 1
        h0 = jnp.broadcast_to(h0_ref[0, d:d + 1, :], (SUBLANES, BRANCH_W))
        if not reverse:
            def emit(r, hh):
                hs_ref[pl.ds(r, SUBLANES), :] = hh
        else:
            def emit(r, hh):
                tot = hs_ref[pl.ds(r, SUBLANES), :] + hh
                y_ref[pl.ds(r, SUBLANES), :] = (tot * jax.nn.gelu(lg_ref[pl.ds(r, SUBLANES), :])).astype(BF16)
        h_end = _lru_scan(a_ref.at[d], b_ref.at[d], h0, nblk, reverse, emit)
        hl_ref[0, d:d + 1, :] = h_end[0:1, :]


def _lru(p, L, layer, conv_w, conv_b, w_rg, b_rg, w_ig, b_ig, lam, h0):
    nb = p.shape[0] // L
    gate_w = _layer_spec(layer, 2, N_GROUPS, GROUP_W, GROUP_W)
    dir_vec = _layer_spec(layer, 2, 1, BRANCH_W)
    return pl.pallas_call(
        functools.partial(_lru_body, L=L),
        grid=(nb,),
        in_specs=[
            pl.BlockSpec((L, BRANCH_W), lambda b: (b, S_LRU)),
            pl.BlockSpec((L, BRANCH_W), lambda b: (b, S_LG)),
            _layer_spec(layer, CONV_W, BRANCH_W),
            _layer_spec(layer, 1, BRANCH_W),
            gate_w, dir_vec, gate_w, dir_vec, dir_vec,
            pl.BlockSpec((1, 2, BRANCH_W), lambda b: (b, 0, 0)),
        ],
        out_specs=[
            pl.BlockSpec((L, BRANCH_W), lambda b: (b, 0)),
            pl.BlockSpec((1, 2, BRANCH_W), lambda b: (b, 0, 0)),
        ],
        out_shape=[
            jax.ShapeDtypeStruct((p.shape[0], BRANCH_W), BF16),
            jax.ShapeDtypeStruct((nb, 2, BRANCH_W), F32),
        ],
        scratch_shapes=[
            pltpu.VMEM((L + 2 * CONV_HALO, BRANCH_W), F32),
            pltpu.VMEM((2, L, BRANCH_W), F32),
            pltpu.VMEM((2, L, BRANCH_W), F32),
            pltpu.VMEM((L, BRANCH_W), F32),
        ],
        compiler_params=_params("parallel"),
        name="lru",
    )(p, p, conv_w, conv_b, w_rg, b_rg, w_ig, b_ig, lam, h0)


_NT = (((1,), (1,)), ((), ()))


def _softmax_pv(scores, values):
    m = scores[0].max(axis=-1, keepdims=True)
    for s in scores[1:]:
        m = jnp.maximum(m, s.max(axis=-1, keepdims=True))
    es = [jnp.exp2(s - m) for s in scores]
    tot = es[0].sum(axis=-1, keepdims=True)
    for e in es[1:]:
        tot = tot + e.sum(axis=-1, keepdims=True)
    out = None
    for e, v in zip(es, values):
        o = jnp.dot(e.astype(BF16), v, preferred_element_type=F32)
        out = o if out is None else out + o
    return out * (1.0 / tot)


def _nattn_body(q_ref, k_ref, v_ref, kc_ref, vc_ref, t_ref, o_ref):
    rows = SEQ // GRID_W
    nq = NA_GROUP * GRID_W
    nk = NA_KEY_ROWS * GRID_W
    scale = GROUP_W ** -0.5 * LOG2_E
    kc = kc_ref[...]
    vc = vc_ref[...]

    def in_window(kr, row0):
        return ((kr >= row0) & (kr < row0 + NA_WIN_R)).astype(jnp.int32)

    def body(gi, carry):
        r0 = gi * NA_GROUP
        base = jnp.clip(r0 - NA_WIN_R // 2, 0, rows - NA_KEY_ROWS)
        qs = pl.multiple_of(r0 * GRID_W, nq)
        ks = pl.multiple_of(base * GRID_W, GRID_W)
        q = q_ref[pl.ds(qs, nq), :]
        kw = k_ref[pl.ds(ks, nk), :]
        vw = v_ref[pl.ds(ks, nk), :]
        s = lax.dot_general(q, kw, _NT, preferred_element_type=F32) * scale
        bias_rows = []
        for g in range(NA_GROUP):
            r = r0 + g
            row0 = jnp.clip(r - NA_WIN_R // 2, 0, rows - NA_WIN_R)
            pieces = []
            for jj in range(NA_KEY_ROWS // 2):
                kr = base + 2 * jj
                variant = in_window(kr, row0) + 2 * in_window(kr + 1, row0)
                e = jnp.clip(kr - r + NA_WIN_R, 0, NA_D - 1)
                pieces.append(t_ref[variant * NA_D + e])
            bias_rows.append(jnp.concatenate(pieces, axis=1))
        bias = jnp.concatenate(bias_rows, axis=0)
        s = jnp.where(bias > 0.5 * NEG_INF, s + bias, NEG_INF)
        sc = lax.dot_general(q, kc, _NT, preferred_element_type=F32) * scale
        o_ref[pl.ds(qs, nq), :] = _softmax_pv([s, sc], [vw, vc]).astype(BF16)
        return carry

    lax.fori_loop(0, rows // NA_GROUP, body, 0, unroll=2)


def _nattn(qkv, qkv_c, layer, bias_table):
    return pl.pallas_call(
        _nattn_body,
        grid=(BATCH, N_GROUPS),
        in_specs=[
            pl.BlockSpec((SEQ, GROUP_W), lambda b, h: (b, Q_COL + h)),
            pl.BlockSpec((SEQ, GROUP_W), lambda b, h: (b, K_COL + h)),
            pl.BlockSpec((SEQ, GROUP_W), lambda b, h: (b, V_COL + h)),
            pl.BlockSpec((CTX_LEN, GROUP_W), lambda b, h: (b, K_COL + h)),
            pl.BlockSpec((CTX_LEN, GROUP_W), lambda b, h: (b, V_COL + h)),
            pl.BlockSpec((None, None, 4 * NA_D, GRID_W, 2 * GRID_W), lambda b, h: (layer, h, 0, 0, 0)),
        ],
        out_specs=pl.BlockSpec((SEQ, GROUP_W), lambda b, h: (b, h)),
        out_shape=jax.ShapeDtypeStruct((qkv.shape[0], BRANCH_W), BF16),
        compiler_params=_params("parallel", "parallel"),
        name="nattn",
    )(qkv, qkv, qkv, qkv_c, qkv_c, bias_table)


def _cattn_body(q_ref, k_ref, v_ref, o_ref):
    scale = GROUP_W ** -0.5 * LOG2_E
    s = lax.dot_general(q_ref[...], k_ref[...], _NT, preferred_element_type=F32) * scale
    o_ref[...] = _softmax_pv([s], [v_ref[...]]).astype(BF16)


def _cattn(qkv_c):
    return pl.pallas_call(
        _cattn_body,
        grid=(BATCH, N_GROUPS),
        in_specs=[
            pl.BlockSpec((CTX_LEN, GROUP_W), lambda b, h: (b, Q_COL + h)),
            pl.BlockSpec((CTX_LEN, GROUP_W), lambda b, h: (b, K_COL + h)),
            pl.BlockSpec((CTX_LEN, GROUP_W), lambda b, h: (b, V_COL + h)),
        ],
        out_specs=pl.BlockSpec((CTX_LEN, GROUP_W), lambda b, h: (b, h)),
        out_shape=jax.ShapeDtypeStruct((qkv_c.shape[0], BRANCH_W), BF16),
        compiler_params=_params("parallel", "parallel"),
        name="cattn",
    )(qkv_c, qkv_c, qkv_c)


def _bias_table(rpb):
    col = jnp.arange(GRID_W)
    col_start = jnp.clip(col - NA_WIN_C // 2, 0, GRID_W - NA_WIN_C)
    col_ok = (col[None, :] >= col_start[:, None]) & (col[None, :] < col_start[:, None] + NA_WIN_C)
    pad = GRID_W - NA_WIN_C
    period = 2 * GRID_W - 1
    u = jnp.pad(rpb.astype(F32), [(0, 0)] * 3 + [(pad, pad)])
    skew = jnp.tile(u, GRID_W + 1)[..., :GRID_W * (period + 1)]
    skew = skew.reshape(rpb.shape[:3] + (GRID_W, period + 1))
    toe = skew[..., ::-1, :GRID_W]
    tab = jnp.where(col_ok, toe * LOG2_E, NEG_INF)
    off = jnp.full_like(tab[:, :, :1], NEG_INF)
    lo = jnp.concatenate([off, tab], axis=2)
    hi = jnp.concatenate([tab, off], axis=2)
    none = jnp.full_like(lo, NEG_INF)
    variants = [jnp.concatenate([lo if v & 1 else none, hi if v & 2 else none], axis=-1) for v in range(4)]
    return jnp.concatenate(variants, axis=2)


def _merge_body(h_ref, y0_ref, y1_ref, y2_ref, y3_ref, g0_ref, g1_ref, g2_ref, g3_ref, wb_ref, o_ref):
    h = h_ref[...]
    acc = None
    for n, (y_ref, g_ref) in enumerate(((y0_ref, g0_ref), (y1_ref, g1_ref), (y2_ref, g2_ref), (y3_ref, g3_ref))):
        gate = jax.nn.sigmoid(jnp.dot(h, g_ref[...], preferred_element_type=F32))
        term = gate * jnp.dot(y_ref[...], wb_ref[n], preferred_element_type=F32)
        acc = term if acc is None else acc + term
    o_ref[...] = acc.astype(BF16)


def _merge(h, ys, layer, w_gate, w_branch):
    rows = h.shape[0]
    tm, tn = TM_MERGE, 512
    nj = D_MODEL // tn
    gate_spec = lambda n: pl.BlockSpec((D_MODEL, tn), lambda i, j: (0, n * nj + j))
    y_spec = pl.BlockSpec((tm, BRANCH_W), lambda i, j: (i, 0))
    return pl.pallas_call(
        _merge_body,
        grid=(rows // tm, nj),
        in_specs=[pl.BlockSpec((tm, D_MODEL), lambda i, j: (i, 0)), y_spec, y_spec, y_spec, y_spec,
                  gate_spec(0), gate_spec(1), gate_spec(2), gate_spec(3),
                  pl.BlockSpec((None, N_BRANCH, BRANCH_W, tn), lambda i, j: (layer, 0, 0, j))],
        out_specs=pl.BlockSpec((tm, tn), lambda i, j: (i, j)),
        out_shape=jax.ShapeDtypeStruct((rows, D_MODEL), BF16),
        compiler_params=_params("parallel", "arbitrary"),
        name="merge",
    )(h, *ys, w_gate, w_gate, w_gate, w_gate, w_branch)


def _outproj_body(m_ref, x_ref, w_ref, gt_ref, g_ref, sh_ref, sc_ref, xm_ref, h2_ref, gs_ref, *, tm):
    xm_ref[...] = x_ref[...] + gt_ref[...] * jnp.dot(m_ref[...], w_ref[...], preferred_element_type=F32)
    _rms_mod_rows(xm_ref, h2_ref, g_ref, sc_ref, sh_ref, gs_ref, tm)


def _outproj(merged, x, mods, layer, row_of_tile, w_out, g_ffn):
    rows = x.shape[0]
    tm = TM_OUT
    row_spec = pl.BlockSpec((tm, D_MODEL), lambda i: (i, 0))
    return pl.pallas_call(
        functools.partial(_outproj_body, tm=tm),
        grid=(rows // tm,),
        in_specs=[row_spec, row_spec,
                  _layer_spec(layer, D_MODEL, D_MODEL),
                  _mod_spec(layer, 2, row_of_tile(tm)),
                  _layer_spec(layer, 1, D_MODEL),
                  _mod_spec(layer, 3, row_of_tile(tm)),
                  _mod_spec(layer, 4, row_of_tile(tm))],
        out_specs=[row_spec, row_spec],
        out_shape=[jax.ShapeDtypeStruct((rows, D_MODEL), F32),
                   jax.ShapeDtypeStruct((rows, D_MODEL), BF16)],
        scratch_shapes=[pltpu.VMEM((1, D_MODEL), F32)],
        compiler_params=_params("parallel"),
        name="outproj",
    )(merged, x, w_out, mods, g_ffn, mods, mods)


def _ffn_body(h_ref, wg_ref, wu_ref, wd_ref, xm_ref, gt_ref, gf_ref, o_ref, a_ref, rs_ref, *, tm, tf, final):
    f = pl.program_id(1)

    @pl.when(f == 0)
    def _():
        o_ref[...] = jnp.zeros_like(o_ref)

    h = h_ref[...]
    for s in range(tf // FFN_SUB):
        sl = slice(s * FFN_SUB, (s + 1) * FFN_SUB)
        g = jnp.dot(h, wg_ref[:, sl], preferred_element_type=F32)
        u = jnp.dot(h, wu_ref[:, sl], preferred_element_type=F32)
        a_ref[:, sl] = (jax.nn.silu(g) * u).astype(BF16)
        o_ref[...] += jnp.dot(a_ref[:, sl], wd_ref[sl, :], preferred_element_type=F32)

    @pl.when(f == pl.num_programs(1) - 1)
    def _():
        def residual(c, carry):
            r = pl.multiple_of(c * NORM_ROWS, NORM_ROWS)
            y = xm_ref[pl.ds(r, NORM_ROWS), :] + gt_ref[...] * o_ref[pl.ds(r, NORM_ROWS), :]
            o_ref[pl.ds(r, NORM_ROWS), :] = y
            if final:
                ms = jnp.mean(y * y, axis=-1, keepdims=True)
                rs_ref[pl.ds(r, NORM_ROWS), :] = jnp.broadcast_to(lax.rsqrt(ms + EPS), (NORM_ROWS, LANES))
            return carry
        lax.fori_loop(0, tm // NORM_ROWS, residual, 0, unroll=4)

        if final:
            def scale(c, carry):
                r = pl.multiple_of(c * NORM_ROWS, NORM_ROWS)
                rs = jnp.concatenate([rs_ref[pl.ds(r, NORM_ROWS), :]] * (D_MODEL // LANES), axis=1)
                o_ref[pl.ds(r, NORM_ROWS), :] = (o_ref[pl.ds(r, NORM_ROWS), :] * rs) * gf_ref[...]
                return carry
            lax.fori_loop(0, tm // NORM_ROWS, scale, 0, unroll=4)


def _ffn(h2, xm, mods, layer, row_of_tile, w_g, w_u, w_d, g_final, final):
    rows = xm.shape[0]
    tm, tf = TM_FFN, 512
    row_spec = pl.BlockSpec((tm, D_MODEL), lambda i, f: (i, 0))
    return pl.pallas_call(
        functools.partial(_ffn_body, tm=tm, tf=tf, final=final),
        grid=(rows // tm, D_FF // tf),
        in_specs=[row_spec,
                  pl.BlockSpec((None, D_MODEL, tf), lambda i, f: (layer, 0, f)),
                  pl.BlockSpec((None, D_MODEL, tf), lambda i, f: (layer, 0, f)),
                  pl.BlockSpec((None, tf, D_MODEL), lambda i, f: (layer, f, 0)),
                  row_spec,
                  _mod_spec(layer, 5, row_of_tile(tm)),
                  pl.BlockSpec((1, D_MODEL), lambda i, f: (0, 0))],
        out_specs=row_spec,
        out_shape=jax.ShapeDtypeStruct((rows, D_MODEL), F32),
        scratch_shapes=[pltpu.VMEM((tm, tf), BF16), pltpu.VMEM((tm, LANES), F32)],
        compiler_params=_params("parallel", "arbitrary"),
        name="ffn",
    )(h2, w_g, w_u, w_d, xm, mods, g_final)


def kernel(x, c, ctx, c_ctx, w_mod, b_mod, g_mix, g_ffn, g_final, w_in, w_pool, pool_scale, gmlp_ln_g, w_sp, b_sp, conv_w, conv_b, w_rg, b_rg, w_ig, b_ig, lru_lam, rpb, w_branch, w_gate, w_out, w_ffn_gate, w_ffn_up, w_ffn_down):
    B, L, D = x.shape
    C = ctx.shape[1]
    assert (B, L, D, C) == (BATCH, SEQ, D_MODEL, CTX_LEN)
    xs = x.reshape(B * L, D)
    cs = ctx.reshape(B * C, D)

    cond = jnp.concatenate([c, c_ctx[None, :], jnp.zeros((COND_ROWS - B - 1, D), F32)], axis=0)
    mods = _adaln_all(cond, w_mod, b_mod)

    w_in_b = w_in.astype(BF16)
    w_gate_b = [w_gate[l].astype(BF16).reshape(D, N_BRANCH * D) for l in range(DEPTH)]
    w_branch_b = w_branch.astype(BF16)
    w_out_b = w_out.astype(BF16)
    w_fg, w_fu, w_fd = w_ffn_gate.astype(BF16), w_ffn_up.astype(BF16), w_ffn_down.astype(BF16)
    g_mix_r = g_mix.reshape(DEPTH, 1, D)
    g_ffn_r = g_ffn.reshape(DEPTH, 1, D)
    g_fin_r = g_final.reshape(1, D)
    pool_args = (w_pool.astype(BF16), pool_scale.reshape(DEPTH, 1, BRANCH_W))
    sgu_args = (gmlp_ln_g.reshape(DEPTH, 1, BRANCH_W), w_sp.astype(BF16), b_sp.reshape(DEPTH, N_GROUPS, CHUNK, 1))
    lru_args = (conv_w, conv_b.reshape(DEPTH, 1, BRANCH_W),
                w_rg.astype(BF16), b_rg.reshape(DEPTH, 2, 1, BRANCH_W),
                w_ig.astype(BF16), b_ig.reshape(DEPTH, 2, 1, BRANCH_W),
                lru_lam.reshape(DEPTH, 2, 1, BRANCH_W))
    bias_table = _bias_table(rpb)
    h_zero = jnp.zeros((B, 2, BRANCH_W), F32)

    for l in range(DEPTH):
        last = l == DEPTH - 1
        px, qkv, hx = _inproj(xs, mods, l, _latent_row, g_mix_r, w_in_b)
        ctx_slices = (S_LRU, S_LG, S_K, S_VAL) if last else tuple(range(N_IN_SLICES))
        pc, qkv_c, hc = _inproj(cs, mods, l, _context_row, g_mix_r, w_in_b, ctx_slices)

        yc_lru, hc_state = _lru(pc, C, l, *lru_args, h_zero)

        y_pool = _pool(px, L, l, *pool_args)
        y_sgu = _sgu(px, L, l, *sgu_args)
        y_lru, _ = _lru(px, L, l, *lru_args, hc_state)
        y_na = _nattn(qkv, qkv_c, l, bias_table)
        merged = _merge(hx, (y_pool, y_sgu, y_lru, y_na), l, w_gate_b[l], w_branch_b)
        xm, hx2 = _outproj(merged, xs, mods, l, _latent_row, w_out_b, g_ffn_r)

        if not last:
            yc_pool = _pool(pc, C, l, *pool_args)
            yc_sgu = _sgu(pc, C, l, *sgu_args)
            yc_na = _cattn(qkv_c)
            merged_c = _merge(hc, (yc_pool, yc_sgu, yc_lru, yc_na), l, w_gate_b[l], w_branch_b)
            cm, hc2 = _outproj(merged_c, cs, mods, l, _context_row, w_out_b, g_ffn_r)
            cs = _ffn(hc2, cm, mods, l, _context_row, w_fg, w_fu, w_fd, g_fin_r, False)

        xs = _ffn(hx2, xm, mods, l, _latent_row, w_fg, w_fu, w_fd, g_fin_r, last)

    return xs.reshape(B, L, D)
```

```python
import functools

import jax
import jax.numpy as jnp
from jax import lax
from jax.experimental import pallas as pl
from jax.experimental.pallas import tpu as pltpu

D_MODEL = 2048
BATCH = 8
SEQ = 2048
DEPTH = 2
CTX_LEN = 256
GRID_W = 64
N_BRANCH = 4
BRANCH_W = D_MODEL // N_BRANCH
GROUP_W = 128
N_GROUPS = BRANCH_W // GROUP_W
N_IN_SLICES = 8
IN_W = N_IN_SLICES * BRANCH_W
POOL_WINDOWS = (2, 4, 8, 16)
CHUNK = 128
CONV_W = 4
LRU_C = 8.0
NA_WIN_R = 8
NA_WIN_C = 16
D_FF = 5632
EPS = 1e-6
NEG_INF = -1e30
LOG2_E = 1.4426950408889634

N_MOD = 6
COND_ROWS = 16
CTX_COND_ROW = BATCH
SUBLANES = 8
LANES = 128
POOL_HALO = 16
CONV_HALO = 8
VMEM_LIMIT = 60 * 1024 * 1024

F32 = jnp.float32
BF16 = jnp.bfloat16

S_POOL, S_U, S_V, S_LRU, S_LG, S_Q, S_K, S_VAL = range(N_IN_SLICES)
N_F32_SLICES = S_Q
HEADS_PER_SLICE = BRANCH_W // GROUP_W
Q_COL, K_COL, V_COL = (HEADS_PER_SLICE * (s - N_F32_SLICES) for s in (S_Q, S_K, S_VAL))
NA_GROUP = 4
NA_KEY_ROWS = 12
NA_D = 2 * NA_WIN_R
FFN_SUB = 256
NORM_ROWS = 16
NORM_GROUP = 8

TM_IN, TM_MERGE, TM_OUT, TM_FFN = 1024, 1024, 512, 1024


def _params(*sem):
    return pltpu.CompilerParams(dimension_semantics=sem, vmem_limit_bytes=VMEM_LIMIT)


def _layer_spec(layer, *shape):
    zeros = (0,) * len(shape)
    return pl.BlockSpec((None,) + shape, lambda *_: (layer,) + zeros)


def _mod_spec(layer, which, row_of_tile):
    def index(i, *_):
        return ((layer * COND_ROWS + row_of_tile(i)) * N_MOD + which, 0, 0)
    return pl.BlockSpec((None, 1, D_MODEL), index)


def _latent_row(tm):
    return lambda i: i // (SEQ // tm)


def _context_row(tm):
    return lambda i: CTX_COND_ROW


def _rms_mod_rows(src_ref, dst_ref, g_ref, sc_ref, sh_ref, gs_ref, rows):
    gs_ref[...] = g_ref[...] * (1.0 + sc_ref[...])
    group = NORM_GROUP * NORM_ROWS

    def body(c, carry):
        base = pl.multiple_of(c * group, group)
        scales = []
        for k in range(NORM_GROUP):
            x = src_ref[pl.ds(base + k * NORM_ROWS, NORM_ROWS), :]
            scales.append(lax.rsqrt(jnp.mean(x * x, axis=-1, keepdims=True) + EPS))
        for k in range(NORM_GROUP):
            x = src_ref[pl.ds(base + k * NORM_ROWS, NORM_ROWS), :]
            y = (x * scales[k]) * gs_ref[...] + sh_ref[...]
            dst_ref[pl.ds(base + k * NORM_ROWS, NORM_ROWS), :] = y.astype(dst_ref.dtype)
        return carry
    lax.fori_loop(0, rows // group, body, 0)


def _mod_body(c_ref, w_ref, b_ref, o_ref):
    @pl.when(pl.program_id(1) == 0)
    def _():
        o_ref[...] = jnp.broadcast_to(b_ref[...], o_ref.shape)
    a = jax.nn.silu(c_ref[...]).astype(BF16)
    o_ref[...] += jnp.dot(a, w_ref[...].astype(BF16), preferred_element_type=F32)


def _adaln_all(cond, w_mod, b_mod):
    tk = 256
    n = N_MOD * D_MODEL
    out = pl.pallas_call(
        _mod_body,
        grid=(DEPTH, D_MODEL // tk),
        in_specs=[
            pl.BlockSpec((COND_ROWS, tk), lambda l, k: (0, k)),
            pl.BlockSpec((None, tk, n), lambda l, k: (l, k, 0)),
            pl.BlockSpec((None, 1, n), lambda l, k: (l, 0, 0)),
        ],
        out_specs=pl.BlockSpec((None, COND_ROWS, n), lambda l, k: (l, 0, 0)),
        out_shape=jax.ShapeDtypeStruct((DEPTH, COND_ROWS, n), F32),
        compiler_params=_params("parallel", "arbitrary"),
        name="adaln",
    )(cond, w_mod, b_mod.reshape(DEPTH, 1, n))
    return out.reshape(DEPTH * COND_ROWS * N_MOD, 1, D_MODEL)


def _inproj_body(x_ref, sh_ref, sc_ref, g_ref, w_ref, a_ref, b_ref, c_ref, h_ref, gs_ref, *, tm, first_pair):
    j = pl.program_id(1)
    pair = j + first_pair
    lo, hi = slice(0, BRANCH_W), slice(BRANCH_W, 2 * BRANCH_W)

    @pl.when(j == 0)
    def _():
        _rms_mod_rows(x_ref, h_ref, g_ref, sc_ref, sh_ref, gs_ref, tm)

    def project(cols):
        return jnp.dot(h_ref[...], w_ref[:, cols], preferred_element_type=F32)

    @pl.when(pair < 2)
    def _():
        a_ref[:, lo] = project(lo)
        a_ref[:, hi] = project(hi)

    @pl.when(pair == 2)
    def _():
        b_ref[...] = project(lo)
        c_ref[:, lo] = project(hi).astype(BF16)

    @pl.when(pair == 3)
    def _():
        c_ref[:, BRANCH_W:2 * BRANCH_W] = project(lo).astype(BF16)
        c_ref[:, 2 * BRANCH_W:3 * BRANCH_W] = project(hi).astype(BF16)


def _inproj(x, mods, layer, row_of_tile, g_mix, w_in, first_pair=0):
    rows = x.shape[0]
    tm, tn = TM_IN, 2 * BRANCH_W
    n_pairs = N_IN_SLICES // 2
    a_pairs = 2 - first_pair
    assert a_pairs >= 1
    return pl.pallas_call(
        functools.partial(_inproj_body, tm=tm, first_pair=first_pair),
        grid=(rows // tm, n_pairs - first_pair),
        in_specs=[
            pl.BlockSpec((tm, D_MODEL), lambda i, j: (i, 0)),
            _mod_spec(layer, 0, row_of_tile(tm)),
            _mod_spec(layer, 1, row_of_tile(tm)),
            _layer_spec(layer, 1, D_MODEL),
            pl.BlockSpec((None, D_MODEL, tn), lambda i, j: (layer, 0, j + first_pair)),
        ],
        out_specs=[
            pl.BlockSpec((tm, tn), lambda i, j: (i, jnp.minimum(j, a_pairs - 1))),
            pl.BlockSpec((tm, BRANCH_W), lambda i, j: (i, 0)),
            pl.BlockSpec((tm, 3 * BRANCH_W), lambda i, j: (i, 0)),
            pl.BlockSpec((tm, D_MODEL), lambda i, j: (i, 0)),
        ],
        out_shape=[
            jax.ShapeDtypeStruct((rows, a_pairs * tn), F32),
            jax.ShapeDtypeStruct((rows, BRANCH_W), F32),
            jax.ShapeDtypeStruct((rows, 3 * BRANCH_W), BF16),
            jax.ShapeDtypeStruct((rows, D_MODEL), BF16),
        ],
        scratch_shapes=[pltpu.VMEM((1, D_MODEL), F32)],
        compiler_params=_params("parallel", "arbitrary"),
        name="inproj",
    )(x, mods, mods, g_mix, w_in)


def _pool_body(z_ref, w_ref, s_ref, o_ref, zp_ref, *, L):
    halo = jnp.zeros((POOL_HALO, BRANCH_W), F32)
    zp_ref[0:POOL_HALO, :] = halo
    zp_ref[POOL_HALO + L:2 * POOL_HALO + L, :] = halo
    zp_ref[POOL_HALO:POOL_HALO + L, :] = z_ref[...]
    rc = CHUNK
    for c in range(L // rc):
        r0 = c * rc
        t = r0 + lax.broadcasted_iota(jnp.int32, (rc, GROUP_W), 0)
        for gi, win in enumerate(POOL_WINDOWS):
            half = win // 2
            sl = slice(gi * GROUP_W, (gi + 1) * GROUP_W)
            acc = zp_ref[POOL_HALO + r0 - half:POOL_HALO + r0 - half + rc, sl]
            for o in range(-half + 1, half):
                acc = acc + zp_ref[POOL_HALO + r0 + o:POOL_HALO + r0 + o + rc, sl]
            cnt = (jnp.minimum(t + half, L) - jnp.maximum(t - half, 0)).astype(F32)
            pooled = acc / cnt - z_ref[r0:r0 + rc, sl]
            y = jnp.dot(pooled.astype(BF16), w_ref[gi], preferred_element_type=F32)
            o_ref[r0:r0 + rc, sl] = (y * s_ref[:, sl]).astype(BF16)


def _pool(p, L, layer, w_pool, pool_scale):
    nb = p.shape[0] // L
    return pl.pallas_call(
        functools.partial(_pool_body, L=L),
        grid=(nb,),
        in_specs=[
            pl.BlockSpec((L, BRANCH_W), lambda b: (b, S_POOL)),
            _layer_spec(layer, N_GROUPS, GROUP_W, GROUP_W),
            _layer_spec(layer, 1, BRANCH_W),
        ],
        out_specs=pl.BlockSpec((L, BRANCH_W), lambda b: (b, 0)),
        out_shape=jax.ShapeDtypeStruct((p.shape[0], BRANCH_W), BF16),
        scratch_shapes=[pltpu.VMEM((L + 2 * POOL_HALO, BRANCH_W), F32)],
        compiler_params=_params("parallel"),
        name="pool",
    )(p, w_pool, pool_scale)


def _sgu_body(u_ref, v_ref, g_ref, w_ref, b_ref, o_ref, *, L):
    for n in range(L // CHUNK):
        r0 = n * CHUNK
        v = v_ref[r0:r0 + CHUNK, :]
        vc = v - jnp.mean(v, axis=-1, keepdims=True)
        var = jnp.mean(vc * vc, axis=-1, keepdims=True)
        vn = ((vc * lax.rsqrt(var + EPS)) * g_ref[...]).astype(BF16)
        for gi in range(N_GROUPS):
            sl = slice(gi * GROUP_W, (gi + 1) * GROUP_W)
            mixed = jnp.dot(w_ref[gi], vn[:, sl], preferred_element_type=F32) + b_ref[gi]
            o_ref[r0:r0 + CHUNK, sl] = (u_ref[r0:r0 + CHUNK, sl] * mixed).astype(BF16)


def _sgu(p, L, layer, ln_g, w_sp, b_sp):
    nb = p.shape[0] // L
    return pl.pallas_call(
        functools.partial(_sgu_body, L=L),
        grid=(nb,),
        in_specs=[
            pl.BlockSpec((L, BRANCH_W), lambda b: (b, S_U)),
            pl.BlockSpec((L, BRANCH_W), lambda b: (b, S_V)),
            _layer_spec(layer, 1, BRANCH_W),
            _layer_spec(layer, N_GROUPS, CHUNK, CHUNK),
            _layer_spec(layer, N_GROUPS, CHUNK, 1),
        ],
        out_specs=pl.BlockSpec((L, BRANCH_W), lambda b: (b, 0)),
        out_shape=jax.ShapeDtypeStruct((p.shape[0], BRANCH_W), BF16),
        compiler_params=_params("parallel"),
        name="sgu",
    )(p, p, ln_g, w_sp, b_sp)


def _lru_scan(a_ref, b_ref, h0, nblk, reverse, emit):
    row = lax.broadcasted_iota(jnp.int32, (SUBLANES, BRANCH_W), 0)

    def body(jj, h):
        j = (nblk - 1 - jj) if reverse else jj
        r = pl.multiple_of(j * SUBLANES, SUBLANES)
        a = a_ref[pl.ds(r, SUBLANES), :]
        b = b_ref[pl.ds(r, SUBLANES), :]
        for s in (1, 2, 4):
            shift = (SUBLANES - s) if reverse else s
            a_s = pltpu.roll(a, shift, 0)
            b_s = pltpu.roll(b, shift, 0)
            m = (row < SUBLANES - s) if reverse else (row >= s)
            b = jnp.where(m, a * b_s + b, b)
            a = jnp.where(m, a * a_s, a)
        hh = a * h + b
        emit(r, hh)
        edge = hh[0:1, :] if reverse else hh[SUBLANES - 1:SUBLANES, :]
        return jnp.broadcast_to(edge, (SUBLANES, BRANCH_W))

    return lax.fori_loop(0, nblk, body, h0, unroll=4)


def _lru_body(z_ref, lg_ref, cw_ref, cb_ref, wr_ref, br_ref, wi_ref, bi_ref, lam_ref, h0_ref,
              y_ref, hl_ref, zp_ref, a_ref, b_ref, hs_ref, *, L):
    halo = jnp.zeros((CONV_HALO, BRANCH_W), F32)
    zp_ref[0:CONV_HALO, :] = halo
    zp_ref[CONV_HALO + L:2 * CONV_HALO + L, :] = halo
    zp_ref[CONV_HALO:CONV_HALO + L, :] = z_ref[...]
    left = CONV_W // 2
    rc = CHUNK
    nblk = L // SUBLANES

    def conv_chunk(r0):
        out = zp_ref[CONV_HALO + r0 - left:CONV_HALO + r0 - left + rc, :] * cw_ref[0:1, :] + cb_ref[...]
        for k in range(1, CONV_W):
            lo = CONV_HALO + r0 - left + k
            out = out + zp_ref[lo:lo + rc, :] * cw_ref[k:k + 1, :]
        return out

    def sigmoid(t):
        return 0.5 * jnp.tanh(0.5 * t) + 0.5

    sp = [jax.nn.softplus(-lam_ref[d]) for d in range(2)]
    for c in range(L // rc):
        r0 = c * rc
        cv = conv_chunk(r0)
        cvb = cv.astype(BF16)
        for d in range(2):
            for gi in range(N_GROUPS):
                sl = slice(gi * GROUP_W, (gi + 1) * GROUP_W)
                rg = sigmoid(jnp.dot(cvb[:, sl], wr_ref[d, gi], preferred_element_type=F32) + br_ref[d][:, sl])
                ig = sigmoid(jnp.dot(cvb[:, sl], wi_ref[d, gi], preferred_element_type=F32) + bi_ref[d][:, sl])
                log_a = (-LRU_C * rg) * sp[d][:, sl]
                a = jnp.exp(log_a)
                a_ref[d, r0:r0 + rc, sl] = a
                one_minus_a2 = -jnp.tanh(log_a) * (a * a + 1.0)
                b_ref[d, r0:r0 + rc, sl] = jnp.sqrt(one_minus_a2) * (ig * cv[:, sl])

    for d in range(2):
        reverse = d == 1
        h0 = jnp.broadcast_to(h0_ref[0, d:d + 1, :], (SUBLANES, BRANCH_W))
        if not reverse:
            def emit(r, hh):
                hs_ref[pl.ds(r, SUBLANES), :] = hh
        else:
            def emit(r, hh):
                tot = hs_ref[pl.ds(r, SUBLANES), :] + hh
                y_ref[pl.ds(r, SUBLANES), :] = (tot * jax.nn.gelu(lg_ref[pl.ds(r, SUBLANES), :])).astype(BF16)
        h_end = _lru_scan(a_ref.at[d], b_ref.at[d], h0, nblk, reverse, emit)
        hl_ref[0, d:d + 1, :] = h_end[0:1, :]


def _lru(p, lru_col, gate, L, layer, conv_w, conv_b, w_rg, b_rg, w_ig, b_ig, lam, h0):
    nb = p.shape[0] // L
    gate_w = _layer_spec(layer, 2, N_GROUPS, GROUP_W, GROUP_W)
    dir_vec = _layer_spec(layer, 2, 1, BRANCH_W)
    return pl.pallas_call(
        functools.partial(_lru_body, L=L),
        grid=(nb,),
        in_specs=[
            pl.BlockSpec((L, BRANCH_W), lambda b: (b, lru_col)),
            pl.BlockSpec((L, BRANCH_W), lambda b: (b, 0)),
            _layer_spec(layer, CONV_W, BRANCH_W),
            _layer_spec(layer, 1, BRANCH_W),
            gate_w, dir_vec, gate_w, dir_vec, dir_vec,
            pl.BlockSpec((1, 2, BRANCH_W), lambda b: (b, 0, 0)),
        ],
        out_specs=[
            pl.BlockSpec((L, BRANCH_W), lambda b: (b, 0)),
            pl.BlockSpec((1, 2, BRANCH_W), lambda b: (b, 0, 0)),
        ],
        out_shape=[
            jax.ShapeDtypeStruct((p.shape[0], BRANCH_W), BF16),
            jax.ShapeDtypeStruct((nb, 2, BRANCH_W), F32),
        ],
        scratch_shapes=[
            pltpu.VMEM((L + 2 * CONV_HALO, BRANCH_W), F32),
            pltpu.VMEM((2, L, BRANCH_W), F32),
            pltpu.VMEM((2, L, BRANCH_W), F32),
            pltpu.VMEM((L, BRANCH_W), F32),
        ],
        compiler_params=_params("parallel"),
        name="lru",
    )(p, gate, conv_w, conv_b, w_rg, b_rg, w_ig, b_ig, lam, h0)


_NT = (((1,), (1,)), ((), ()))


def _softmax_pv(scores, values):
    m = scores[0].max(axis=-1, keepdims=True)
    for s in scores[1:]:
        m = jnp.maximum(m, s.max(axis=-1, keepdims=True))
    es = [jnp.exp2(s - m) for s in scores]
    tot = es[0].sum(axis=-1, keepdims=True)
    for e in es[1:]:
        tot = tot + e.sum(axis=-1, keepdims=True)
    out = None
    for e, v in zip(es, values):
        o = jnp.dot(e.astype(BF16), v, preferred_element_type=F32)
        out = o if out is None else out + o
    return out * (1.0 / tot)


def _nattn_body(q_ref, k_ref, v_ref, kc_ref, vc_ref, t_ref, o_ref):
    rows = SEQ // GRID_W
    nq = NA_GROUP * GRID_W
    nk = NA_KEY_ROWS * GRID_W
    scale = GROUP_W ** -0.5 * LOG2_E
    kc = kc_ref[...]
    vc = vc_ref[...]

    def in_window(kr, row0):
        return ((kr >= row0) & (kr < row0 + NA_WIN_R)).astype(jnp.int32)

    def body(gi, carry):
        r0 = gi * NA_GROUP
        base = jnp.clip(r0 - NA_WIN_R // 2, 0, rows - NA_KEY_ROWS)
        qs = pl.multiple_of(r0 * GRID_W, nq)
        ks = pl.multiple_of(base * GRID_W, GRID_W)
        q = q_ref[pl.ds(qs, nq), :]
        kw = k_ref[pl.ds(ks, nk), :]
        vw = v_ref[pl.ds(ks, nk), :]
        s = lax.dot_general(q, kw, _NT, preferred_element_type=F32) * scale
        bias_rows = []
        for g in range(NA_GROUP):
            r = r0 + g
            row0 = jnp.clip(r - NA_WIN_R // 2, 0, rows - NA_WIN_R)
            pieces = []
            for jj in range(NA_KEY_ROWS // 2):
                kr = base + 2 * jj
                variant = in_window(kr, row0) + 2 * in_window(kr + 1, row0)
                e = jnp.clip(kr - r + NA_WIN_R, 0, NA_D - 1)
                pieces.append(t_ref[variant * NA_D + e])
            bias_rows.append(jnp.concatenate(pieces, axis=1))
        bias = jnp.concatenate(bias_rows, axis=0)
        s = jnp.where(bias > 0.5 * NEG_INF, s + bias, NEG_INF)
        sc = lax.dot_general(q, kc, _NT, preferred_element_type=F32) * scale
        o_ref[pl.ds(qs, nq), :] = _softmax_pv([s, sc], [vw, vc]).astype(BF16)
        return carry

    lax.fori_loop(0, rows // NA_GROUP, body, 0, unroll=2)


def _nattn(qkv, qkv_c, layer, bias_table):
    return pl.pallas_call(
        _nattn_body,
        grid=(BATCH, N_GROUPS),
        in_specs=[
            pl.BlockSpec((SEQ, GROUP_W), lambda b, h: (b, Q_COL + h)),
            pl.BlockSpec((SEQ, GROUP_W), lambda b, h: (b, K_COL + h)),
            pl.BlockSpec((SEQ, GROUP_W), lambda b, h: (b, V_COL + h)),
            pl.BlockSpec((CTX_LEN, GROUP_W), lambda b, h: (b, K_COL + h)),
            pl.BlockSpec((CTX_LEN, GROUP_W), lambda b, h: (b, V_COL + h)),
            pl.BlockSpec((None, None, 4 * NA_D, GRID_W, 2 * GRID_W), lambda b, h: (layer, h, 0, 0, 0)),
        ],
        out_specs=pl.BlockSpec((SEQ, GROUP_W), lambda b, h: (b, h)),
        out_shape=jax.ShapeDtypeStruct((qkv.shape[0], BRANCH_W), BF16),
        compiler_params=_params("parallel", "parallel"),
        name="nattn",
    )(qkv, qkv, qkv, qkv_c, qkv_c, bias_table)


def _cattn_body(q_ref, k_ref, v_ref, o_ref):
    scale = GROUP_W ** -0.5 * LOG2_E
    s = lax.dot_general(q_ref[...], k_ref[...], _NT, preferred_element_type=F32) * scale
    o_ref[...] = _softmax_pv([s], [v_ref[...]]).astype(BF16)


def _cattn(qkv_c):
    return pl.pallas_call(
        _cattn_body,
        grid=(BATCH, N_GROUPS),
        in_specs=[
            pl.BlockSpec((CTX_LEN, GROUP_W), lambda b, h: (b, Q_COL + h)),
            pl.BlockSpec((CTX_LEN, GROUP_W), lambda b, h: (b, K_COL + h)),
            pl.BlockSpec((CTX_LEN, GROUP_W), lambda b, h: (b, V_COL + h)),
        ],
        out_specs=pl.BlockSpec((CTX_LEN, GROUP_W), lambda b, h: (b, h)),
        out_shape=jax.ShapeDtypeStruct((qkv_c.shape[0], BRANCH_W), BF16),
        compiler_params=_params("parallel", "parallel"),
        name="cattn",
    )(qkv_c, qkv_c, qkv_c)


def _bias_table(rpb):
    col = jnp.arange(GRID_W)
    col_start = jnp.clip(col - NA_WIN_C // 2, 0, GRID_W - NA_WIN_C)
    col_ok = (col[None, :] >= col_start[:, None]) & (col[None, :] < col_start[:, None] + NA_WIN_C)
    pad = GRID_W - NA_WIN_C
    period = 2 * GRID_W - 1
    u = jnp.pad(rpb.astype(F32), [(0, 0)] * 3 + [(pad, pad)])
    skew = jnp.tile(u, GRID_W + 1)[..., :GRID_W * (period + 1)]
    skew = skew.reshape(rpb.shape[:3] + (GRID_W, period + 1))
    toe = skew[..., ::-1, :GRID_W]
    tab = jnp.where(col_ok, toe * LOG2_E, NEG_INF)
    off = jnp.full_like(tab[:, :, :1], NEG_INF)
    lo = jnp.concatenate([off, tab], axis=2)
    hi = jnp.concatenate([tab, off], axis=2)
    none = jnp.full_like(lo, NEG_INF)
    variants = [jnp.concatenate([lo if v & 1 else none, hi if v & 2 else none], axis=-1) for v in range(4)]
    return jnp.concatenate(variants, axis=2)


def _merge_body(h_ref, y0_ref, y1_ref, y2_ref, y3_ref, g0_ref, g1_ref, g2_ref, g3_ref, wb_ref, o_ref):
    h = h_ref[...]
    acc = None
    for n, (y_ref, g_ref) in enumerate(((y0_ref, g0_ref), (y1_ref, g1_ref), (y2_ref, g2_ref), (y3_ref, g3_ref))):
        gate = jax.nn.sigmoid(jnp.dot(h, g_ref[...], preferred_element_type=F32))
        term = gate * jnp.dot(y_ref[...], wb_ref[n], preferred_element_type=F32)
        acc = term if acc is None else acc + term
    o_ref[...] = acc.astype(BF16)


def _merge(h, ys, layer, w_gate, w_branch):
    rows = h.shape[0]
    tm, tn = TM_MERGE, 512
    nj = D_MODEL // tn
    gate_spec = lambda n: pl.BlockSpec((D_MODEL, tn), lambda i, j: (0, n * nj + j))
    y_spec = pl.BlockSpec((tm, BRANCH_W), lambda i, j: (i, 0))
    return pl.pallas_call(
        _merge_body,
        grid=(rows // tm, nj),
        in_specs=[pl.BlockSpec((tm, D_MODEL), lambda i, j: (i, 0)), y_spec, y_spec, y_spec, y_spec,
                  gate_spec(0), gate_spec(1), gate_spec(2), gate_spec(3),
                  pl.BlockSpec((None, N_BRANCH, BRANCH_W, tn), lambda i, j: (layer, 0, 0, j))],
        out_specs=pl.BlockSpec((tm, tn), lambda i, j: (i, j)),
        out_shape=jax.ShapeDtypeStruct((rows, D_MODEL), BF16),
        compiler_params=_params("parallel", "arbitrary"),
        name="merge",
    )(h, *ys, w_gate, w_gate, w_gate, w_gate, w_branch)


def _outproj_body(m_ref, x_ref, w_ref, gt_ref, g_ref, sh_ref, sc_ref, xm_ref, h2_ref, gs_ref, *, tm):
    xm_ref[...] = x_ref[...] + gt_ref[...] * jnp.dot(m_ref[...], w_ref[...], preferred_element_type=F32)
    _rms_mod_rows(xm_ref, h2_ref, g_ref, sc_ref, sh_ref, gs_ref, tm)


def _outproj(merged, x, mods, layer, row_of_tile, w_out, g_ffn):
    rows = x.shape[0]
    tm = TM_OUT
    row_spec = pl.BlockSpec((tm, D_MODEL), lambda i: (i, 0))
    return pl.pallas_call(
        functools.partial(_outproj_body, tm=tm),
        grid=(rows // tm,),
        in_specs=[row_spec, row_spec,
                  _layer_spec(layer, D_MODEL, D_MODEL),
                  _mod_spec(layer, 2, row_of_tile(tm)),
                  _layer_spec(layer, 1, D_MODEL),
                  _mod_spec(layer, 3, row_of_tile(tm)),
                  _mod_spec(layer, 4, row_of_tile(tm))],
        out_specs=[row_spec, row_spec],
        out_shape=[jax.ShapeDtypeStruct((rows, D_MODEL), F32),
                   jax.ShapeDtypeStruct((rows, D_MODEL), BF16)],
        scratch_shapes=[pltpu.VMEM((1, D_MODEL), F32)],
        compiler_params=_params("parallel"),
        name="outproj",
    )(merged, x, w_out, mods, g_ffn, mods, mods)


def _ffn_body(h_ref, wg_ref, wu_ref, wd_ref, xm_ref, gt_ref, gf_ref, o_ref, a_ref, rs_ref, *, tm, tf, final):
    f = pl.program_id(1)

    @pl.when(f == 0)
    def _():
        o_ref[...] = jnp.zeros_like(o_ref)

    h = h_ref[...]
    for s in range(tf // FFN_SUB):
        sl = slice(s * FFN_SUB, (s + 1) * FFN_SUB)
        g = jnp.dot(h, wg_ref[:, sl], preferred_element_type=F32)
        u = jnp.dot(h, wu_ref[:, sl], preferred_element_type=F32)
        a_ref[:, sl] = (jax.nn.silu(g) * u).astype(BF16)
        o_ref[...] += jnp.dot(a_ref[:, sl], wd_ref[sl, :], preferred_element_type=F32)

    @pl.when(f == pl.num_programs(1) - 1)
    def _():
        def residual(c, carry):
            r = pl.multiple_of(c * NORM_ROWS, NORM_ROWS)
            y = xm_ref[pl.ds(r, NORM_ROWS), :] + gt_ref[...] * o_ref[pl.ds(r, NORM_ROWS), :]
            o_ref[pl.ds(r, NORM_ROWS), :] = y
            if final:
                ms = jnp.mean(y * y, axis=-1, keepdims=True)
                rs_ref[pl.ds(r, NORM_ROWS), :] = jnp.broadcast_to(lax.rsqrt(ms + EPS), (NORM_ROWS, LANES))
            return carry
        lax.fori_loop(0, tm // NORM_ROWS, residual, 0, unroll=4)

        if final:
            def scale(c, carry):
                r = pl.multiple_of(c * NORM_ROWS, NORM_ROWS)
                rs = jnp.concatenate([rs_ref[pl.ds(r, NORM_ROWS), :]] * (D_MODEL // LANES), axis=1)
                o_ref[pl.ds(r, NORM_ROWS), :] = (o_ref[pl.ds(r, NORM_ROWS), :] * rs) * gf_ref[...]
                return carry
            lax.fori_loop(0, tm // NORM_ROWS, scale, 0, unroll=4)


def _ffn(h2, xm, mods, layer, row_of_tile, w_g, w_u, w_d, g_final, final):
    rows = xm.shape[0]
    tm, tf = TM_FFN, 512
    row_spec = pl.BlockSpec((tm, D_MODEL), lambda i, f: (i, 0))
    return pl.pallas_call(
        functools.partial(_ffn_body, tm=tm, tf=tf, final=final),
        grid=(rows // tm, D_FF // tf),
        in_specs=[row_spec,
                  pl.BlockSpec((None, D_MODEL, tf), lambda i, f: (layer, 0, f)),
                  pl.BlockSpec((None, D_MODEL, tf), lambda i, f: (layer, 0, f)),
                  pl.BlockSpec((None, tf, D_MODEL), lambda i, f: (layer, f, 0)),
                  row_spec,
                  _mod_spec(layer, 5, row_of_tile(tm)),
                  pl.BlockSpec((1, D_MODEL), lambda i, f: (0, 0))],
        out_specs=row_spec,
        out_shape=jax.ShapeDtypeStruct((rows, D_MODEL), F32),
        scratch_shapes=[pltpu.VMEM((tm, tf), BF16), pltpu.VMEM((tm, LANES), F32)],
        compiler_params=_params("parallel", "arbitrary"),
        name="ffn",
    )(h2, w_g, w_u, w_d, xm, mods, g_final)


def kernel(x, c, ctx, c_ctx, w_mod, b_mod, g_mix, g_ffn, g_final, w_in, w_pool, pool_scale, gmlp_ln_g, w_sp, b_sp, conv_w, conv_b, w_rg, b_rg, w_ig, b_ig, lru_lam, rpb, w_branch, w_gate, w_out, w_ffn_gate, w_ffn_up, w_ffn_down):
    B, L, D = x.shape
    C = ctx.shape[1]
    assert (B, L, D, C) == (BATCH, SEQ, D_MODEL, CTX_LEN)
    xs = x.reshape(B * L, D)
    cs = ctx.reshape(B * C, D)

    cond = jnp.concatenate([c, c_ctx[None, :], jnp.zeros((COND_ROWS - B - 1, D), F32)], axis=0)
    mods = _adaln_all(cond, w_mod, b_mod)

    w_in_b = w_in.astype(BF16)
    w_gate_b = [w_gate[l].astype(BF16).reshape(D, N_BRANCH * D) for l in range(DEPTH)]
    w_branch_b = w_branch.astype(BF16)
    w_out_b = w_out.astype(BF16)
    w_fg, w_fu, w_fd = w_ffn_gate.astype(BF16), w_ffn_up.astype(BF16), w_ffn_down.astype(BF16)
    g_mix_r = g_mix.reshape(DEPTH, 1, D)
    g_ffn_r = g_ffn.reshape(DEPTH, 1, D)
    g_fin_r = g_final.reshape(1, D)
    pool_args = (w_pool.astype(BF16), pool_scale.reshape(DEPTH, 1, BRANCH_W))
    sgu_args = (gmlp_ln_g.reshape(DEPTH, 1, BRANCH_W), w_sp.astype(BF16), b_sp.reshape(DEPTH, N_GROUPS, CHUNK, 1))
    lru_args = (conv_w, conv_b.reshape(DEPTH, 1, BRANCH_W),
                w_rg.astype(BF16), b_rg.reshape(DEPTH, 2, 1, BRANCH_W),
                w_ig.astype(BF16), b_ig.reshape(DEPTH, 2, 1, BRANCH_W),
                lru_lam.reshape(DEPTH, 2, 1, BRANCH_W))
    bias_table = _bias_table(rpb)
    h_zero = jnp.zeros((B, 2, BRANCH_W), F32)

    for l in range(DEPTH):
        last = l == DEPTH - 1
        px, gx, qkv, hx = _inproj(xs, mods, l, _latent_row, g_mix_r, w_in_b)
        ctx_first = 1 if last else 0
        pc, gc, qkv_c, hc = _inproj(cs, mods, l, _context_row, g_mix_r, w_in_b, ctx_first)

        yc_lru, hc_state = _lru(pc, S_LRU - 2 * ctx_first, gc, C, l, *lru_args, h_zero)

        y_pool = _pool(px, L, l, *pool_args)
        y_sgu = _sgu(px, L, l, *sgu_args)
        y_lru, _ = _lru(px, S_LRU, gx, L, l, *lru_args, hc_state)
        y_na = _nattn(qkv, qkv_c, l, bias_table)
        merged = _merge(hx, (y_pool, y_sgu, y_lru, y_na), l, w_gate_b[l], w_branch_b)
        xm, hx2 = _outproj(merged, xs, mods, l, _latent_row, w_out_b, g_ffn_r)

        if not last:
            yc_pool = _pool(pc, C, l, *pool_args)
            yc_sgu = _sgu(pc, C, l, *sgu_args)
            yc_na = _cattn(qkv_c)
            merged_c = _merge(hc, (yc_pool, yc_sgu, yc_lru, yc_na), l, w_gate_b[l], w_branch_b)
            cm, hc2 = _outproj(merged_c, cs, mods, l, _context_row, w_out_b, g_ffn_r)
            cs = _ffn(hc2, cm, mods, l, _context_row, w_fg, w_fu, w_fd, g_fin_r, False)

        xs = _ffn(hx2, xm, mods, l, _latent_row, w_fg, w_fu, w_fd, g_fin_r, last)

    return xs.reshape(B, L, D)
```

```python
import functools

import jax
import jax.numpy as jnp
from jax import lax
from jax.experimental import pallas as pl
from jax.experimental.pallas import tpu as pltpu

D_MODEL = 2048
BATCH = 8
SEQ = 2048
DEPTH = 2
CTX_LEN = 256
GRID_W = 64
N_BRANCH = 4
BRANCH_W = D_MODEL // N_BRANCH
GROUP_W = 128
N_GROUPS = BRANCH_W // GROUP_W
N_IN_SLICES = 8
IN_W = N_IN_SLICES * BRANCH_W
POOL_WINDOWS = (2, 4, 8, 16)
CHUNK = 128
CONV_W = 4
LRU_C = 8.0
NA_WIN_R = 8
NA_WIN_C = 16
D_FF = 5632
EPS = 1e-6
NEG_INF = -1e30
LOG2_E = 1.4426950408889634

N_MOD = 6
COND_ROWS = 16
CTX_COND_ROW = BATCH
SUBLANES = 8
LANES = 128
POOL_HALO = 16
CONV_HALO = 8
VMEM_LIMIT = 60 * 1024 * 1024

F32 = jnp.float32
BF16 = jnp.bfloat16

S_POOL, S_U, S_V, S_LRU, S_LG, S_Q, S_K, S_VAL = range(N_IN_SLICES)
N_F32_SLICES = S_Q
HEADS_PER_SLICE = BRANCH_W // GROUP_W
Q_COL, K_COL, V_COL = (HEADS_PER_SLICE * (s - N_F32_SLICES) for s in (S_Q, S_K, S_VAL))
NA_GROUP = 4
NA_KEY_ROWS = 12
NA_D = 2 * NA_WIN_R
FFN_SUB = 256
NORM_ROWS = 16
NORM_GROUP = 8

TM_IN, TM_MERGE, TM_OUT, TM_FFN = 1024, 1024, 512, 1024


def _params(*sem):
    return pltpu.CompilerParams(dimension_semantics=sem, vmem_limit_bytes=VMEM_LIMIT)


def _layer_spec(layer, *shape):
    zeros = (0,) * len(shape)
    return pl.BlockSpec((None,) + shape, lambda *_: (layer,) + zeros)


def _mod_spec(layer, which, row_of_tile):
    def index(i, *_):
        return ((layer * COND_ROWS + row_of_tile(i)) * N_MOD + which, 0, 0)
    return pl.BlockSpec((None, 1, D_MODEL), index)


def _latent_row(tm):
    return lambda i: i // (SEQ // tm)


def _context_row(tm):
    return lambda i: CTX_COND_ROW


def _rms_mod_rows(src_ref, dst_ref, g_ref, sc_ref, sh_ref, gs_ref, rows):
    gs_ref[...] = g_ref[...] * (1.0 + sc_ref[...])
    group = NORM_GROUP * NORM_ROWS

    def body(c, carry):
        base = pl.multiple_of(c * group, group)
        scales = []
        for k in range(NORM_GROUP):
            x = src_ref[pl.ds(base + k * NORM_ROWS, NORM_ROWS), :]
            scales.append(lax.rsqrt(jnp.mean(x * x, axis=-1, keepdims=True) + EPS))
        for k in range(NORM_GROUP):
            x = src_ref[pl.ds(base + k * NORM_ROWS, NORM_ROWS), :]
            y = (x * scales[k]) * gs_ref[...] + sh_ref[...]
            dst_ref[pl.ds(base + k * NORM_ROWS, NORM_ROWS), :] = y.astype(dst_ref.dtype)
        return carry
    lax.fori_loop(0, rows // group, body, 0)


def _mod_body(c_ref, w_ref, b_ref, o_ref):
    @pl.when(pl.program_id(1) == 0)
    def _():
        o_ref[...] = jnp.broadcast_to(b_ref[...], o_ref.shape)
    a = jax.nn.silu(c_ref[...]).astype(BF16)
    o_ref[...] += jnp.dot(a, w_ref[...].astype(BF16), preferred_element_type=F32)


def _adaln_all(cond, w_mod, b_mod):
    tk = 256
    n = N_MOD * D_MODEL
    out = pl.pallas_call(
        _mod_body,
        grid=(DEPTH, D_MODEL // tk),
        in_specs=[
            pl.BlockSpec((COND_ROWS, tk), lambda l, k: (0, k)),
            pl.BlockSpec((None, tk, n), lambda l, k: (l, k, 0)),
            pl.BlockSpec((None, 1, n), lambda l, k: (l, 0, 0)),
        ],
        out_specs=pl.BlockSpec((None, COND_ROWS, n), lambda l, k: (l, 0, 0)),
        out_shape=jax.ShapeDtypeStruct((DEPTH, COND_ROWS, n), F32),
        compiler_params=_params("parallel", "arbitrary"),
        name="adaln",
    )(cond, w_mod, b_mod.reshape(DEPTH, 1, n))
    return out.reshape(DEPTH * COND_ROWS * N_MOD, 1, D_MODEL)


def _inproj_body(x_ref, sh_ref, sc_ref, g_ref, w_ref, a_ref, b_ref, c_ref, h_ref, gs_ref, *, tm, first_pair):
    j = pl.program_id(1)
    pair = j + first_pair
    lo, hi = slice(0, BRANCH_W), slice(BRANCH_W, 2 * BRANCH_W)

    @pl.when(j == 0)
    def _():
        _rms_mod_rows(x_ref, h_ref, g_ref, sc_ref, sh_ref, gs_ref, tm)

    def project(cols):
        return jnp.dot(h_ref[...], w_ref[:, cols], preferred_element_type=F32)

    @pl.when(pair < 2)
    def _():
        a_ref[:, lo] = project(lo)
        a_ref[:, hi] = project(hi)

    @pl.when(pair == 2)
    def _():
        b_ref[...] = project(lo)
        c_ref[:, lo] = project(hi).astype(BF16)

    @pl.when(pair == 3)
    def _():
        c_ref[:, BRANCH_W:2 * BRANCH_W] = project(lo).astype(BF16)
        c_ref[:, 2 * BRANCH_W:3 * BRANCH_W] = project(hi).astype(BF16)


def _inproj(x, mods, layer, row_of_tile, g_mix, w_in, first_pair=0):
    rows = x.shape[0]
    tm, tn = TM_IN, 2 * BRANCH_W
    n_pairs = N_IN_SLICES // 2
    a_pairs = 2 - first_pair
    assert a_pairs >= 1
    return pl.pallas_call(
        functools.partial(_inproj_body, tm=tm, first_pair=first_pair),
        grid=(rows // tm, n_pairs - first_pair),
        in_specs=[
            pl.BlockSpec((tm, D_MODEL), lambda i, j: (i, 0)),
            _mod_spec(layer, 0, row_of_tile(tm)),
            _mod_spec(layer, 1, row_of_tile(tm)),
            _layer_spec(layer, 1, D_MODEL),
            pl.BlockSpec((None, D_MODEL, tn), lambda i, j: (layer, 0, j + first_pair)),
        ],
        out_specs=[
            pl.BlockSpec((tm, tn), lambda i, j: (i, jnp.minimum(j, a_pairs - 1))),
            pl.BlockSpec((tm, BRANCH_W), lambda i, j: (i, 0)),
            pl.BlockSpec((tm, 3 * BRANCH_W), lambda i, j: (i, 0)),
            pl.BlockSpec((tm, D_MODEL), lambda i, j: (i, 0)),
        ],
        out_shape=[
            jax.ShapeDtypeStruct((rows, a_pairs * tn), F32),
            jax.ShapeDtypeStruct((rows, BRANCH_W), F32),
            jax.ShapeDtypeStruct((rows, 3 * BRANCH_W), BF16),
            jax.ShapeDtypeStruct((rows, D_MODEL), BF16),
        ],
        scratch_shapes=[pltpu.VMEM((1, D_MODEL), F32)],
        compiler_params=_params("parallel", "arbitrary"),
        name="inproj",
    )(x, mods, mods, g_mix, w_in)


def _pool_body(z_ref, w_ref, s_ref, o_ref, zp_ref, *, L):
    halo = jnp.zeros((POOL_HALO, BRANCH_W), F32)
    zp_ref[0:POOL_HALO, :] = halo
    zp_ref[POOL_HALO + L:2 * POOL_HALO + L, :] = halo
    zp_ref[POOL_HALO:POOL_HALO + L, :] = z_ref[...]
    rc = CHUNK
    for c in range(L // rc):
        r0 = c * rc
        t = r0 + lax.broadcasted_iota(jnp.int32, (rc, GROUP_W), 0)
        for gi, win in enumerate(POOL_WINDOWS):
            half = win // 2
            sl = slice(gi * GROUP_W, (gi + 1) * GROUP_W)
            acc = zp_ref[POOL_HALO + r0 - half:POOL_HALO + r0 - half + rc, sl]
            for o in range(-half + 1, half):
                acc = acc + zp_ref[POOL_HALO + r0 + o:POOL_HALO + r0 + o + rc, sl]
            cnt = (jnp.minimum(t + half, L) - jnp.maximum(t - half, 0)).astype(F32)
            pooled = acc / cnt - z_ref[r0:r0 + rc, sl]
            y = jnp.dot(pooled.astype(BF16), w_ref[gi], preferred_element_type=F32)
            o_ref[r0:r0 + rc, sl] = (y * s_ref[:, sl]).astype(BF16)


def _pool(p, L, layer, w_pool, pool_scale):
    nb = p.shape[0] // L
    return pl.pallas_call(
        functools.partial(_pool_body, L=L),
        grid=(nb,),
        in_specs=[
            pl.BlockSpec((L, BRANCH_W), lambda b: (b, S_POOL)),
            _layer_spec(layer, N_GROUPS, GROUP_W, GROUP_W),
            _layer_spec(layer, 1, BRANCH_W),
        ],
        out_specs=pl.BlockSpec((L, BRANCH_W), lambda b: (b, 0)),
        out_shape=jax.ShapeDtypeStruct((p.shape[0], BRANCH_W), BF16),
        scratch_shapes=[pltpu.VMEM((L + 2 * POOL_HALO, BRANCH_W), F32)],
        compiler_params=_params("parallel"),
        name="pool",
    )(p, w_pool, pool_scale)


def _sgu_body(u_ref, v_ref, g_ref, w_ref, b_ref, o_ref, *, L):
    for n in range(L // CHUNK):
        r0 = n * CHUNK
        v = v_ref[r0:r0 + CHUNK, :]
        vc = v - jnp.mean(v, axis=-1, keepdims=True)
        var = jnp.mean(vc * vc, axis=-1, keepdims=True)
        vn = ((vc * lax.rsqrt(var + EPS)) * g_ref[...]).astype(BF16)
        for gi in range(N_GROUPS):
            sl = slice(gi * GROUP_W, (gi + 1) * GROUP_W)
            mixed = jnp.dot(w_ref[gi], vn[:, sl], preferred_element_type=F32) + b_ref[gi]
            o_ref[r0:r0 + CHUNK, sl] = (u_ref[r0:r0 + CHUNK, sl] * mixed).astype(BF16)


def _sgu(p, L, layer, ln_g, w_sp, b_sp):
    nb = p.shape[0] // L
    return pl.pallas_call(
        functools.partial(_sgu_body, L=L),
        grid=(nb,),
        in_specs=[
            pl.BlockSpec((L, BRANCH_W), lambda b: (b, S_U)),
            pl.BlockSpec((L, BRANCH_W), lambda b: (b, S_V)),
            _layer_spec(layer, 1, BRANCH_W),
            _layer_spec(layer, N_GROUPS, CHUNK, CHUNK),
            _layer_spec(layer, N_GROUPS, CHUNK, 1),
        ],
        out_specs=pl.BlockSpec((L, BRANCH_W), lambda b: (b, 0)),
        out_shape=jax.ShapeDtypeStruct((p.shape[0], BRANCH_W), BF16),
        compiler_params=_params("parallel"),
        name="sgu",
    )(p, p, ln_g, w_sp, b_sp)


def _lru_scan(a_ref, b_ref, h0, nblk, reverse, emit):
    row = lax.broadcasted_iota(jnp.int32, (SUBLANES, BRANCH_W), 0)

    def body(jj, h):
        j = (nblk - 1 - jj) if reverse else jj
        r = pl.multiple_of(j * SUBLANES, SUBLANES)
        a = a_ref[pl.ds(r, SUBLANES), :]
        b = b_ref[pl.ds(r, SUBLANES), :]
        for s in (1, 2, 4):
            shift = (SUBLANES - s) if reverse else s
            a_s = pltpu.roll(a, shift, 0)
            b_s = pltpu.roll(b, shift, 0)
            m = (row < SUBLANES - s) if reverse else (row >= s)
            b = jnp.where(m, a * b_s + b, b)
            a = jnp.where(m, a * a_s, a)
        hh = a * h + b
        emit(r, hh)
        edge = hh[0:1, :] if reverse else hh[SUBLANES - 1:SUBLANES, :]
        return jnp.broadcast_to(edge, (SUBLANES, BRANCH_W))

    return lax.fori_loop(0, nblk, body, h0, unroll=4)


def _lru_body(z_ref, lg_ref, cw_ref, cb_ref, wr_ref, br_ref, wi_ref, bi_ref, lam_ref, h0_ref,
              y_ref, hl_ref, zp_ref, a_ref, b_ref, hs_ref, *, L):
    halo = jnp.zeros((CONV_HALO, BRANCH_W), F32)
    zp_ref[0:CONV_HALO, :] = halo
    zp_ref[CONV_HALO + L:2 * CONV_HALO + L, :] = halo
    zp_ref[CONV_HALO:CONV_HALO + L, :] = z_ref[...]
    left = CONV_W // 2
    rc = CHUNK
    nblk = L // SUBLANES

    def conv_chunk(r0):
        out = zp_ref[CONV_HALO + r0 - left:CONV_HALO + r0 - left + rc, :] * cw_ref[0:1, :] + cb_ref[...]
        for k in range(1, CONV_W):
            lo = CONV_HALO + r0 - left + k
            out = out + zp_ref[lo:lo + rc, :] * cw_ref[k:k + 1, :]
        return out

    def sigmoid(t):
        return 0.5 * jnp.tanh(0.5 * t) + 0.5

    sp = [jax.nn.softplus(-lam_ref[d]) for d in range(2)]
    for c in range(L // rc):
        r0 = c * rc
        cv = conv_chunk(r0)
        cvb = cv.astype(BF16)
        for d in range(2):
            for gi in range(N_GROUPS):
                sl = slice(gi * GROUP_W, (gi + 1) * GROUP_W)
                rg = sigmoid(jnp.dot(cvb[:, sl], wr_ref[d, gi], preferred_element_type=F32) + br_ref[d][:, sl])
                ig = sigmoid(jnp.dot(cvb[:, sl], wi_ref[d, gi], preferred_element_type=F32) + bi_ref[d][:, sl])
                log_a = (-LRU_C * rg) * sp[d][:, sl]
                a = jnp.exp(log_a)
                a_ref[d, r0:r0 + rc, sl] = a
                one_minus_a2 = -jnp.tanh(log_a) * (a * a + 1.0)
                b_ref[d, r0:r0 + rc, sl] = jnp.sqrt(one_minus_a2) * (ig * cv[:, sl])

    for d in range(2):
        reverse = d == 1
        h0 = jnp.broadcast_to(h0_ref[0, d:d + 1, :], (SUBLANES, BRANCH_W))
        if not reverse:
            def emit(r, hh):
                hs_ref[pl.ds(r, SUBLANES), :] = hh
        else:
            def emit(r, hh):
                tot = hs_ref[pl.ds(r, SUBLANES), :] + hh
                y_ref[pl.ds(r, SUBLANES), :] = (tot * jax.nn.gelu(lg_ref[pl.ds(r, SUBLANES), :])).astype(BF16)
        h_end = _lru_scan(a_ref.at[d], b_ref.at[d], h0, nblk, reverse, emit)
        hl_ref[0, d:d + 1, :] = h_end[0:1, :]


def _lru(p, lru_col, gate, L, layer, conv_w, conv_b, w_rg, b_rg, w_ig, b_ig, lam, h0):
    nb = p.shape[0] // L
    gate_w = _layer_spec(layer, 2, N_GROUPS, GROUP_W, GROUP_W)
    dir_vec = _layer_spec(layer, 2, 1, BRANCH_W)
    return pl.pallas_call(
        functools.partial(_lru_body, L=L),
        grid=(nb,),
        in_specs=[
            pl.BlockSpec((L, BRANCH_W), lambda b: (b, lru_col)),
            pl.BlockSpec((L, BRANCH_W), lambda b: (b, 0)),
            _layer_spec(layer, CONV_W, BRANCH_W),
            _layer_spec(layer, 1, BRANCH_W),
            gate_w, dir_vec, gate_w, dir_vec, dir_vec,
            pl.BlockSpec((1, 2, BRANCH_W), lambda b: (b, 0, 0)),
        ],
        out_specs=[
            pl.BlockSpec((L, BRANCH_W), lambda b: (b, 0)),
            pl.BlockSpec((1, 2, BRANCH_W), lambda b: (b, 0, 0)),
        ],
        out_shape=[
            jax.ShapeDtypeStruct((p.shape[0], BRANCH_W), BF16),
            jax.ShapeDtypeStruct((nb, 2, BRANCH_W), F32),
        ],
        scratch_shapes=[
            pltpu.VMEM((L + 2 * CONV_HALO, BRANCH_W), F32),
            pltpu.VMEM((2, L, BRANCH_W), F32),
            pltpu.VMEM((2, L, BRANCH_W), F32),
            pltpu.VMEM((L, BRANCH_W), F32),
        ],
        compiler_params=_params("parallel"),
        name="lru",
    )(p, gate, conv_w, conv_b, w_rg, b_rg, w_ig, b_ig, lam, h0)


_NT = (((1,), (1,)), ((), ()))


def _softmax_pv(scores, values):
    m = scores[0].max(axis=-1, keepdims=True)
    for s in scores[1:]:
        m = jnp.maximum(m, s.max(axis=-1, keepdims=True))
    es = [jnp.exp2(s - m) for s in scores]
    tot = es[0].sum(axis=-1, keepdims=True)
    for e in es[1:]:
        tot = tot + e.sum(axis=-1, keepdims=True)
    out = None
    for e, v in zip(es, values):
        o = jnp.dot(e.astype(BF16), v, preferred_element_type=F32)
        out = o if out is None else out + o
    return out * (1.0 / tot)


def _nattn_body(q_ref, k_ref, v_ref, kc_ref, vc_ref, t_ref, o_ref):
    rows = SEQ // GRID_W
    nq = NA_GROUP * GRID_W
    nk = NA_KEY_ROWS * GRID_W
    scale = GROUP_W ** -0.5 * LOG2_E
    kc = kc_ref[...]
    vc = vc_ref[...]

    def in_window(kr, row0):
        return ((kr >= row0) & (kr < row0 + NA_WIN_R)).astype(jnp.int32)

    def body(gi, carry):
        r0 = gi * NA_GROUP
        base = jnp.clip(r0 - NA_WIN_R // 2, 0, rows - NA_KEY_ROWS)
        qs = pl.multiple_of(r0 * GRID_W, nq)
        ks = pl.multiple_of(base * GRID_W, GRID_W)
        q = q_ref[pl.ds(qs, nq), :]
        kw = k_ref[pl.ds(ks, nk), :]
        vw = v_ref[pl.ds(ks, nk), :]
        s = lax.dot_general(q, kw, _NT, preferred_element_type=F32) * scale
        bias_rows = []
        for g in range(NA_GROUP):
            r = r0 + g
            row0 = jnp.clip(r - NA_WIN_R // 2, 0, rows - NA_WIN_R)
            pieces = []
            for jj in range(NA_KEY_ROWS // 2):
                kr = base + 2 * jj
                variant = in_window(kr, row0) + 2 * in_window(kr + 1, row0)
                e = jnp.clip(kr - r + NA_WIN_R, 0, NA_D - 1)
                pieces.append(t_ref[variant * NA_D + e])
            bias_rows.append(jnp.concatenate(pieces, axis=1))
        bias = jnp.concatenate(bias_rows, axis=0)
        s = jnp.where(bias > 0.5 * NEG_INF, s + bias, NEG_INF)
        sc = lax.dot_general(q, kc, _NT, preferred_element_type=F32) * scale
        o_ref[pl.ds(qs, nq), :] = _softmax_pv([s, sc], [vw, vc]).astype(BF16)
        return carry

    lax.fori_loop(0, rows // NA_GROUP, body, 0, unroll=2)


def _nattn(qkv, qkv_c, layer, bias_table):
    return pl.pallas_call(
        _nattn_body,
        grid=(BATCH, N_GROUPS),
        in_specs=[
            pl.BlockSpec((SEQ, GROUP_W), lambda b, h: (b, Q_COL + h)),
            pl.BlockSpec((SEQ, GROUP_W), lambda b, h: (b, K_COL + h)),
            pl.BlockSpec((SEQ, GROUP_W), lambda b, h: (b, V_COL + h)),
            pl.BlockSpec((CTX_LEN, GROUP_W), lambda b, h: (b, K_COL + h)),
            pl.BlockSpec((CTX_LEN, GROUP_W), lambda b, h: (b, V_COL + h)),
            pl.BlockSpec((None, None, 4 * NA_D, GRID_W, 2 * GRID_W), lambda b, h: (layer, h, 0, 0, 0)),
        ],
        out_specs=pl.BlockSpec((SEQ, GROUP_W), lambda b, h: (b, h)),
        out_shape=jax.ShapeDtypeStruct((qkv.shape[0], BRANCH_W), BF16),
        compiler_params=_params("parallel", "parallel"),
        name="nattn",
    )(qkv, qkv, qkv, qkv_c, qkv_c, bias_table)


def _cattn_body(q_ref, k_ref, v_ref, o_ref):
    scale = GROUP_W ** -0.5 * LOG2_E
    s = lax.dot_general(q_ref[...], k_ref[...], _NT, preferred_element_type=F32) * scale
    o_ref[...] = _softmax_pv([s], [v_ref[...]]).astype(BF16)


def _cattn(qkv_c):
    return pl.pallas_call(
        _cattn_body,
        grid=(BATCH, N_GROUPS),
        in_specs=[
            pl.BlockSpec((CTX_LEN, GROUP_W), lambda b, h: (b, Q_COL + h)),
            pl.BlockSpec((CTX_LEN, GROUP_W), lambda b, h: (b, K_COL + h)),
            pl.BlockSpec((CTX_LEN, GROUP_W), lambda b, h: (b, V_COL + h)),
        ],
        out_specs=pl.BlockSpec((CTX_LEN, GROUP_W), lambda b, h: (b, h)),
        out_shape=jax.ShapeDtypeStruct((qkv_c.shape[0], BRANCH_W), BF16),
        compiler_params=_params("parallel", "parallel"),
        name="cattn",
    )(qkv_c, qkv_c, qkv_c)


def _bias_table(rpb):
    col = jnp.arange(GRID_W)
    col_start = jnp.clip(col - NA_WIN_C // 2, 0, GRID_W - NA_WIN_C)
    col_ok = (col[None, :] >= col_start[:, None]) & (col[None, :] < col_start[:, None] + NA_WIN_C)
    pad = GRID_W - NA_WIN_C
    period = 2 * GRID_W - 1
    u = jnp.pad(rpb.astype(F32), [(0, 0)] * 3 + [(pad, pad)])
    skew = jnp.tile(u, GRID_W + 1)[..., :GRID_W * (period + 1)]
    skew = skew.reshape(rpb.shape[:3] + (GRID_W, period + 1))
    toe = skew[..., ::-1, :GRID_W]
    tab = jnp.where(col_ok, toe * LOG2_E, NEG_INF)
    off = jnp.full_like(tab[:, :, :1], NEG_INF)
    lo = jnp.concatenate([off, tab], axis=2)
    hi = jnp.concatenate([tab, off], axis=2)
    none = jnp.full_like(lo, NEG_INF)
    variants = [jnp.concatenate([lo if v & 1 else none, hi if v & 2 else none], axis=-1) for v in range(4)]
    return jnp.concatenate(variants, axis=2)


def _merge_body(h_ref, y0_ref, y1_ref, y2_ref, y3_ref, g0_ref, g1_ref, g2_ref, g3_ref, wb_ref, o_ref):
    h = h_ref[...]
    acc = None
    for n, (y_ref, g_ref) in enumerate(((y0_ref, g0_ref), (y1_ref, g1_ref), (y2_ref, g2_ref), (y3_ref, g3_ref))):
        gate = jax.nn.sigmoid(jnp.dot(h, g_ref[...], preferred_element_type=F32))
        term = gate * jnp.dot(y_ref[...], wb_ref[n], preferred_element_type=F32)
        acc = term if acc is None else acc + term
    o_ref[...] = acc.astype(BF16)


def _merge(h, ys, layer, w_gate, w_branch):
    rows = h.shape[0]
    tm, tn = TM_MERGE, 512
    nj = D_MODEL // tn
    gate_spec = lambda n: pl.BlockSpec((D_MODEL, tn), lambda i, j: (0, n * nj + j))
    y_spec = pl.BlockSpec((tm, BRANCH_W), lambda i, j: (i, 0))
    return pl.pallas_call(
        _merge_body,
        grid=(rows // tm, nj),
        in_specs=[pl.BlockSpec((tm, D_MODEL), lambda i, j: (i, 0)), y_spec, y_spec, y_spec, y_spec,
                  gate_spec(0), gate_spec(1), gate_spec(2), gate_spec(3),
                  pl.BlockSpec((None, N_BRANCH, BRANCH_W, tn), lambda i, j: (layer, 0, 0, j))],
        out_specs=pl.BlockSpec((tm, tn), lambda i, j: (i, j)),
        out_shape=jax.ShapeDtypeStruct((rows, D_MODEL), BF16),
        compiler_params=_params("parallel", "arbitrary"),
        name="merge",
    )(h, *ys, w_gate, w_gate, w_gate, w_gate, w_branch)


def _outproj_body(m_ref, x_ref, w_ref, gt_ref, g_ref, sh_ref, sc_ref, xm_ref, h2_ref, gs_ref, *, tm):
    xm_ref[...] = x_ref[...] + gt_ref[...] * jnp.dot(m_ref[...], w_ref[...], preferred_element_type=F32)
    _rms_mod_rows(xm_ref, h2_ref, g_ref, sc_ref, sh_ref, gs_ref, tm)


def _outproj(merged, x, mods, layer, row_of_tile, w_out, g_ffn):
    rows = x.shape[0]
    tm = TM_OUT
    row_spec = pl.BlockSpec((tm, D_MODEL), lambda i: (i, 0))
    return pl.pallas_call(
        functools.partial(_outproj_body, tm=tm),
        grid=(rows // tm,),
        in_specs=[row_spec, row_spec,
                  _layer_spec(layer, D_MODEL, D_MODEL),
                  _mod_spec(layer, 2, row_of_tile(tm)),
                  _layer_spec(layer, 1, D_MODEL),
                  _mod_spec(layer, 3, row_of_tile(tm)),
                  _mod_spec(layer, 4, row_of_tile(tm))],
        out_specs=[row_spec, row_spec],
        out_shape=[jax.ShapeDtypeStruct((rows, D_MODEL), F32),
                   jax.ShapeDtypeStruct((rows, D_MODEL), BF16)],
        scratch_shapes=[pltpu.VMEM((1, D_MODEL), F32)],
        compiler_params=_params("parallel"),
        name="outproj",
    )(merged, x, w_out, mods, g_ffn, mods, mods)


def _ffn_body(h_ref, wg_ref, wu_ref, wd_ref, xm_ref, gt_ref, gf_ref, o_ref, a_ref, rs_ref, *, tm, tf, final):
    f = pl.program_id(1)

    @pl.when(f == 0)
    def _():
        o_ref[...] = jnp.zeros_like(o_ref)

    h = h_ref[...]
    for s in range(tf // FFN_SUB):
        sl = slice(s * FFN_SUB, (s + 1) * FFN_SUB)
        g = jnp.dot(h, wg_ref[:, sl], preferred_element_type=F32)
        u = jnp.dot(h, wu_ref[:, sl], preferred_element_type=F32)
        a_ref[:, sl] = (jax.nn.silu(g) * u).astype(BF16)
        o_ref[...] += jnp.dot(a_ref[:, sl], wd_ref[sl, :], preferred_element_type=F32)

    @pl.when(f == pl.num_programs(1) - 1)
    def _():
        def residual(c, carry):
            r = pl.multiple_of(c * NORM_ROWS, NORM_ROWS)
            y = xm_ref[pl.ds(r, NORM_ROWS), :] + gt_ref[...] * o_ref[pl.ds(r, NORM_ROWS), :]
            o_ref[pl.ds(r, NORM_ROWS), :] = y
            if final:
                ms = jnp.mean(y * y, axis=-1, keepdims=True)
                rs_ref[pl.ds(r, NORM_ROWS), :] = jnp.broadcast_to(lax.rsqrt(ms + EPS), (NORM_ROWS, LANES))
            return carry
        lax.fori_loop(0, tm // NORM_ROWS, residual, 0, unroll=4)

        if final:
            def scale(c, carry):
                r = pl.multiple_of(c * NORM_ROWS, NORM_ROWS)
                rs = jnp.concatenate([rs_ref[pl.ds(r, NORM_ROWS), :]] * (D_MODEL // LANES), axis=1)
                o_ref[pl.ds(r, NORM_ROWS), :] = (o_ref[pl.ds(r, NORM_ROWS), :] * rs) * gf_ref[...]
                return carry
            lax.fori_loop(0, tm // NORM_ROWS, scale, 0, unroll=4)


def _ffn(h2, xm, mods, layer, row_of_tile, w_g, w_u, w_d, g_final, final):
    rows = xm.shape[0]
    tm, tf = TM_FFN, 512
    row_spec = pl.BlockSpec((tm, D_MODEL), lambda i, f: (i, 0))
    return pl.pallas_call(
        functools.partial(_ffn_body, tm=tm, tf=tf, final=final),
        grid=(rows // tm, D_FF // tf),
        in_specs=[row_spec,
                  pl.BlockSpec((None, D_MODEL, tf), lambda i, f: (layer, 0, f)),
                  pl.BlockSpec((None, D_MODEL, tf), lambda i, f: (layer, 0, f)),
                  pl.BlockSpec((None, tf, D_MODEL), lambda i, f: (layer, f, 0)),
                  row_spec,
                  _mod_spec(layer, 5, row_of_tile(tm)),
                  pl.BlockSpec((1, D_MODEL), lambda i, f: (0, 0))],
        out_specs=row_spec,
        out_shape=jax.ShapeDtypeStruct((rows, D_MODEL), F32),
        scratch_shapes=[pltpu.VMEM((tm, tf), BF16), pltpu.VMEM((tm, LANES), F32)],
        compiler_params=_params("parallel", "arbitrary"),
        name="ffn",
    )(h2, w_g, w_u, w_d, xm, mods, g_final)


def kernel(x, c, ctx, c_ctx, w_mod, b_mod, g_mix, g_ffn, g_final, w_in, w_pool, pool_scale, gmlp_ln_g, w_sp, b_sp, conv_w, conv_b, w_rg, b_rg, w_ig, b_ig, lru_lam, rpb, w_branch, w_gate, w_out, w_ffn_gate, w_ffn_up, w_ffn_down):
    B, L, D = x.shape
    C = ctx.shape[1]
    assert (B, L, D, C) == (BATCH, SEQ, D_MODEL, CTX_LEN)
    xs = x.reshape(B * L, D)
    cs = ctx.reshape(B * C, D)

    cond = jnp.concatenate([c, c_ctx[None, :], jnp.zeros((COND_ROWS - B - 1, D), F32)], axis=0)
    mods = _adaln_all(cond, w_mod, b_mod)
    mods, w_gate = lax.optimization_barrier((mods, w_gate))

    w_in_b = w_in.astype(BF16)
    w_gate_b = [w_gate[l].astype(BF16).reshape(D, N_BRANCH * D) for l in range(DEPTH)]
    w_branch_b = w_branch.astype(BF16)
    w_out_b = w_out.astype(BF16)
    w_fg, w_fu, w_fd = w_ffn_gate.astype(BF16), w_ffn_up.astype(BF16), w_ffn_down.astype(BF16)
    g_mix_r = g_mix.reshape(DEPTH, 1, D)
    g_ffn_r = g_ffn.reshape(DEPTH, 1, D)
    g_fin_r = g_final.reshape(1, D)
    pool_args = (w_pool.astype(BF16), pool_scale.reshape(DEPTH, 1, BRANCH_W))
    sgu_args = (gmlp_ln_g.reshape(DEPTH, 1, BRANCH_W), w_sp.astype(BF16), b_sp.reshape(DEPTH, N_GROUPS, CHUNK, 1))
    lru_args = (conv_w, conv_b.reshape(DEPTH, 1, BRANCH_W),
                w_rg.astype(BF16), b_rg.reshape(DEPTH, 2, 1, BRANCH_W),
                w_ig.astype(BF16), b_ig.reshape(DEPTH, 2, 1, BRANCH_W),
                lru_lam.reshape(DEPTH, 2, 1, BRANCH_W))
    bias_table = _bias_table(rpb)
    h_zero = jnp.zeros((B, 2, BRANCH_W), F32)

    for l in range(DEPTH):
        last = l == DEPTH - 1
        px, gx, qkv, hx = _inproj(xs, mods, l, _latent_row, g_mix_r, w_in_b)
        ctx_first = 1 if last else 0
        pc, gc, qkv_c, hc = _inproj(cs, mods, l, _context_row, g_mix_r, w_in_b, ctx_first)

        yc_lru, hc_state = _lru(pc, S_LRU - 2 * ctx_first, gc, C, l, *lru_args, h_zero)

        y_pool = _pool(px, L, l, *pool_args)
        y_sgu = _sgu(px, L, l, *sgu_args)
        y_lru, _ = _lru(px, S_LRU, gx, L, l, *lru_args, hc_state)
        y_na = _nattn(qkv, qkv_c, l, bias_table)
        merged = _merge(hx, (y_pool, y_sgu, y_lru, y_na), l, w_gate_b[l], w_branch_b)
        xm, hx2 = _outproj(merged, xs, mods, l, _latent_row, w_out_b, g_ffn_r)

        if not last:
            yc_pool = _pool(pc, C, l, *pool_args)
            yc_sgu = _sgu(pc, C, l, *sgu_args)
            yc_na = _cattn(qkv_c)
            merged_c = _merge(hc, (yc_pool, yc_sgu, yc_lru, yc_na), l, w_gate_b[l], w_branch_b)
            cm, hc2 = _outproj(merged_c, cs, mods, l, _context_row, w_out_b, g_ffn_r)
            cs = _ffn(hc2, cm, mods, l, _context_row, w_fg, w_fu, w_fd, g_fin_r, False)

        xs = _ffn(hx2, xm, mods, l, _latent_row, w_fg, w_fu, w_fd, g_fin_r, last)

    return xs.reshape(B, L, D)
```

```python
import functools

import jax
import jax.numpy as jnp
from jax import lax
from jax.experimental import pallas as pl
from jax.experimental.pallas import tpu as pltpu

D_MODEL = 2048
BATCH = 8
SEQ = 2048
DEPTH = 2
CTX_LEN = 256
GRID_W = 64
N_BRANCH = 4
BRANCH_W = D_MODEL // N_BRANCH
GROUP_W = 128
N_GROUPS = BRANCH_W // GROUP_W
N_IN_SLICES = 8
IN_W = N_IN_SLICES * BRANCH_W
POOL_WINDOWS = (2, 4, 8, 16)
CHUNK = 128
CONV_W = 4
LRU_C = 8.0
NA_WIN_R = 8
NA_WIN_C = 16
D_FF = 5632
EPS = 1e-6
NEG_INF = -1e30
LOG2_E = 1.4426950408889634

N_MOD = 6
COND_ROWS = 16
CTX_COND_ROW = BATCH
SUBLANES = 8
LANES = 128
POOL_HALO = 16
CONV_HALO = 8
VMEM_LIMIT = 60 * 1024 * 1024

F32 = jnp.float32
BF16 = jnp.bfloat16

S_POOL, S_U, S_V, S_LRU, S_LG, S_Q, S_K, S_VAL = range(N_IN_SLICES)
N_F32_SLICES = S_Q
HEADS_PER_SLICE = BRANCH_W // GROUP_W
Q_COL, K_COL, V_COL = (HEADS_PER_SLICE * (s - N_F32_SLICES) for s in (S_Q, S_K, S_VAL))
NA_GROUP = 4
NA_KEY_ROWS = 12
NA_D = 2 * NA_WIN_R
FFN_SUB = 256
NORM_ROWS = 16
NORM_GROUP = 8

TM_IN, TM_MERGE, TM_OUT, TM_FFN = 1024, 1024, 512, 1024


def _params(*sem):
    return pltpu.CompilerParams(dimension_semantics=sem, vmem_limit_bytes=VMEM_LIMIT)


def _layer_spec(layer, *shape):
    zeros = (0,) * len(shape)
    return pl.BlockSpec((None,) + shape, lambda *_: (layer,) + zeros)


def _mod_spec(layer, which, row_of_tile):
    def index(i, *_):
        return ((layer * COND_ROWS + row_of_tile(i)) * N_MOD + which, 0, 0)
    return pl.BlockSpec((None, 1, D_MODEL), index)


def _latent_row(tm):
    return lambda i: i // (SEQ // tm)


def _context_row(tm):
    return lambda i: CTX_COND_ROW


def _rms_mod_rows(src_ref, dst_ref, g_ref, sc_ref, sh_ref, gs_ref, rows):
    gs_ref[...] = g_ref[...] * (1.0 + sc_ref[...])
    group = NORM_GROUP * NORM_ROWS

    def body(c, carry):
        base = pl.multiple_of(c * group, group)
        scales = []
        for k in range(NORM_GROUP):
            x = src_ref[pl.ds(base + k * NORM_ROWS, NORM_ROWS), :]
            scales.append(lax.rsqrt(jnp.mean(x * x, axis=-1, keepdims=True) + EPS))
        for k in range(NORM_GROUP):
            x = src_ref[pl.ds(base + k * NORM_ROWS, NORM_ROWS), :]
            y = (x * scales[k]) * gs_ref[...] + sh_ref[...]
            dst_ref[pl.ds(base + k * NORM_ROWS, NORM_ROWS), :] = y.astype(dst_ref.dtype)
        return carry
    lax.fori_loop(0, rows // group, body, 0)


def _mod_body(c_ref, w_ref, b_ref, o_ref):
    @pl.when(pl.program_id(1) == 0)
    def _():
        o_ref[...] = jnp.broadcast_to(b_ref[...], o_ref.shape)
    a = jax.nn.silu(c_ref[...]).astype(BF16)
    o_ref[...] += jnp.dot(a, w_ref[...].astype(BF16), preferred_element_type=F32)


def _adaln_all(cond, w_mod, b_mod):
    tk = 256
    n = N_MOD * D_MODEL
    out = pl.pallas_call(
        _mod_body,
        grid=(DEPTH, D_MODEL // tk),
        in_specs=[
            pl.BlockSpec((COND_ROWS, tk), lambda l, k: (0, k)),
            pl.BlockSpec((None, tk, n), lambda l, k: (l, k, 0)),
            pl.BlockSpec((None, 1, n), lambda l, k: (l, 0, 0)),
        ],
        out_specs=pl.BlockSpec((None, COND_ROWS, n), lambda l, k: (l, 0, 0)),
        out_shape=jax.ShapeDtypeStruct((DEPTH, COND_ROWS, n), F32),
        compiler_params=_params("parallel", "arbitrary"),
        name="adaln",
    )(cond, w_mod, b_mod.reshape(DEPTH, 1, n))
    return out.reshape(DEPTH * COND_ROWS * N_MOD, 1, D_MODEL)


def _inproj_body(x_ref, sh_ref, sc_ref, g_ref, w_ref, a_ref, b_ref, c_ref, h_ref, gs_ref, *, tm, first_pair):
    j = pl.program_id(1)
    pair = j + first_pair
    lo, hi = slice(0, BRANCH_W), slice(BRANCH_W, 2 * BRANCH_W)

    @pl.when(j == 0)
    def _():
        _rms_mod_rows(x_ref, h_ref, g_ref, sc_ref, sh_ref, gs_ref, tm)

    def project(cols):
        return jnp.dot(h_ref[...], w_ref[:, cols], preferred_element_type=F32)

    @pl.when(pair < 2)
    def _():
        a_ref[:, lo] = project(lo)
        a_ref[:, hi] = project(hi)

    @pl.when(pair == 2)
    def _():
        b_ref[...] = project(lo)
        c_ref[:, lo] = project(hi).astype(BF16)

    @pl.when(pair == 3)
    def _():
        c_ref[:, BRANCH_W:2 * BRANCH_W] = project(lo).astype(BF16)
        c_ref[:, 2 * BRANCH_W:3 * BRANCH_W] = project(hi).astype(BF16)


def _inproj(x, mods, layer, row_of_tile, g_mix, w_in, first_pair=0):
    rows = x.shape[0]
    tm, tn = TM_IN, 2 * BRANCH_W
    n_pairs = N_IN_SLICES // 2
    a_pairs = 2 - first_pair
    assert a_pairs >= 1
    return pl.pallas_call(
        functools.partial(_inproj_body, tm=tm, first_pair=first_pair),
        grid=(rows // tm, n_pairs - first_pair),
        in_specs=[
            pl.BlockSpec((tm, D_MODEL), lambda i, j: (i, 0)),
            _mod_spec(layer, 0, row_of_tile(tm)),
            _mod_spec(layer, 1, row_of_tile(tm)),
            _layer_spec(layer, 1, D_MODEL),
            pl.BlockSpec((None, D_MODEL, tn), lambda i, j: (layer, 0, j + first_pair)),
        ],
        out_specs=[
            pl.BlockSpec((tm, tn), lambda i, j: (i, jnp.minimum(j, a_pairs - 1))),
            pl.BlockSpec((tm, BRANCH_W), lambda i, j: (i, 0)),
            pl.BlockSpec((tm, 3 * BRANCH_W), lambda i, j: (i, 0)),
            pl.BlockSpec((tm, D_MODEL), lambda i, j: (i, 0)),
        ],
        out_shape=[
            jax.ShapeDtypeStruct((rows, a_pairs * tn), F32),
            jax.ShapeDtypeStruct((rows, BRANCH_W), F32),
            jax.ShapeDtypeStruct((rows, 3 * BRANCH_W), BF16),
            jax.ShapeDtypeStruct((rows, D_MODEL), BF16),
        ],
        scratch_shapes=[pltpu.VMEM((1, D_MODEL), F32)],
        compiler_params=_params("parallel", "arbitrary"),
        name="inproj",
    )(x, mods, mods, g_mix, w_in)


def _pool_body(z_ref, w_ref, s_ref, o_ref, zp_ref, *, L):
    halo = jnp.zeros((POOL_HALO, BRANCH_W), F32)
    zp_ref[0:POOL_HALO, :] = halo
    zp_ref[POOL_HALO + L:2 * POOL_HALO + L, :] = halo
    zp_ref[POOL_HALO:POOL_HALO + L, :] = z_ref[...]
    rc = CHUNK
    for c in range(L // rc):
        r0 = c * rc
        t = r0 + lax.broadcasted_iota(jnp.int32, (rc, GROUP_W), 0)
        for gi, win in enumerate(POOL_WINDOWS):
            half = win // 2
            sl = slice(gi * GROUP_W, (gi + 1) * GROUP_W)
            acc = zp_ref[POOL_HALO + r0 - half:POOL_HALO + r0 - half + rc, sl]
            for o in range(-half + 1, half):
                acc = acc + zp_ref[POOL_HALO + r0 + o:POOL_HALO + r0 + o + rc, sl]
            cnt = (jnp.minimum(t + half, L) - jnp.maximum(t - half, 0)).astype(F32)
            pooled = acc / cnt - z_ref[r0:r0 + rc, sl]
            y = jnp.dot(pooled.astype(BF16), w_ref[gi], preferred_element_type=F32)
            o_ref[r0:r0 + rc, sl] = (y * s_ref[:, sl]).astype(BF16)


def _pool(p, L, layer, w_pool, pool_scale):
    nb = p.shape[0] // L
    return pl.pallas_call(
        functools.partial(_pool_body, L=L),
        grid=(nb,),
        in_specs=[
            pl.BlockSpec((L, BRANCH_W), lambda b: (b, S_POOL)),
            _layer_spec(layer, N_GROUPS, GROUP_W, GROUP_W),
            _layer_spec(layer, 1, BRANCH_W),
        ],
        out_specs=pl.BlockSpec((L, BRANCH_W), lambda b: (b, 0)),
        out_shape=jax.ShapeDtypeStruct((p.shape[0], BRANCH_W), BF16),
        scratch_shapes=[pltpu.VMEM((L + 2 * POOL_HALO, BRANCH_W), F32)],
        compiler_params=_params("parallel"),
        name="pool",
    )(p, w_pool, pool_scale)


def _sgu_body(u_ref, v_ref, g_ref, w_ref, b_ref, o_ref, *, L):
    for n in range(L // CHUNK):
        r0 = n * CHUNK
        v = v_ref[r0:r0 + CHUNK, :]
        vc = v - jnp.mean(v, axis=-1, keepdims=True)
        var = jnp.mean(vc * vc, axis=-1, keepdims=True)
        vn = ((vc * lax.rsqrt(var + EPS)) * g_ref[...]).astype(BF16)
        for gi in range(N_GROUPS):
            sl = slice(gi * GROUP_W, (gi + 1) * GROUP_W)
            mixed = jnp.dot(w_ref[gi], vn[:, sl], preferred_element_type=F32) + b_ref[gi]
            o_ref[r0:r0 + CHUNK, sl] = (u_ref[r0:r0 + CHUNK, sl] * mixed).astype(BF16)


def _sgu(p, L, layer, ln_g, w_sp, b_sp):
    nb = p.shape[0] // L
    return pl.pallas_call(
        functools.partial(_sgu_body, L=L),
        grid=(nb,),
        in_specs=[
            pl.BlockSpec((L, BRANCH_W), lambda b: (b, S_U)),
            pl.BlockSpec((L, BRANCH_W), lambda b: (b, S_V)),
            _layer_spec(layer, 1, BRANCH_W),
            _layer_spec(layer, N_GROUPS, CHUNK, CHUNK),
            _layer_spec(layer, N_GROUPS, CHUNK, 1),
        ],
        out_specs=pl.BlockSpec((L, BRANCH_W), lambda b: (b, 0)),
        out_shape=jax.ShapeDtypeStruct((p.shape[0], BRANCH_W), BF16),
        compiler_params=_params("parallel"),
        name="sgu",
    )(p, p, ln_g, w_sp, b_sp)


def _lru_scan(a_ref, b_ref, h0, nblk, reverse, emit):
    row = lax.broadcasted_iota(jnp.int32, (SUBLANES, BRANCH_W), 0)

    def body(jj, h):
        j = (nblk - 1 - jj) if reverse else jj
        r = pl.multiple_of(j * SUBLANES, SUBLANES)
        a = a_ref[pl.ds(r, SUBLANES), :]
        b = b_ref[pl.ds(r, SUBLANES), :]
        for s in (1, 2, 4):
            shift = (SUBLANES - s) if reverse else s
            a_s = pltpu.roll(a, shift, 0)
            b_s = pltpu.roll(b, shift, 0)
            m = (row < SUBLANES - s) if reverse else (row >= s)
            b = jnp.where(m, a * b_s + b, b)
            a = jnp.where(m, a * a_s, a)
        hh = a * h + b
        emit(r, hh)
        edge = hh[0:1, :] if reverse else hh[SUBLANES - 1:SUBLANES, :]
        return jnp.broadcast_to(edge, (SUBLANES, BRANCH_W))

    return lax.fori_loop(0, nblk, body, h0, unroll=4)


def _lru_body(z_ref, lg_ref, cw_ref, cb_ref, wr_ref, br_ref, wi_ref, bi_ref, lam_ref, h0_ref,
              y_ref, hl_ref, zp_ref, a_ref, b_ref, hs_ref, *, L):
    halo = jnp.zeros((CONV_HALO, BRANCH_W), F32)
    zp_ref[0:CONV_HALO, :] = halo
    zp_ref[CONV_HALO + L:2 * CONV_HALO + L, :] = halo
    zp_ref[CONV_HALO:CONV_HALO + L, :] = z_ref[...]
    left = CONV_W // 2
    rc = CHUNK
    nblk = L // SUBLANES

    def conv_chunk(r0):
        out = zp_ref[CONV_HALO + r0 - left:CONV_HALO + r0 - left + rc, :] * cw_ref[0:1, :] + cb_ref[...]
        for k in range(1, CONV_W):
            lo = CONV_HALO + r0 - left + k
            out = out + zp_ref[lo:lo + rc, :] * cw_ref[k:k + 1, :]
        return out

    def sigmoid(t):
        return 0.5 * jnp.tanh(0.5 * t) + 0.5

    sp = [jax.nn.softplus(-lam_ref[d]) for d in range(2)]
    for c in range(L // rc):
        r0 = c * rc
        cv = conv_chunk(r0)
        cvb = cv.astype(BF16)
        for d in range(2):
            for gi in range(N_GROUPS):
                sl = slice(gi * GROUP_W, (gi + 1) * GROUP_W)
                rg = sigmoid(jnp.dot(cvb[:, sl], wr_ref[d, gi], preferred_element_type=F32) + br_ref[d][:, sl])
                ig = sigmoid(jnp.dot(cvb[:, sl], wi_ref[d, gi], preferred_element_type=F32) + bi_ref[d][:, sl])
                log_a = (-LRU_C * rg) * sp[d][:, sl]
                a = jnp.exp(log_a)
                a_ref[d, r0:r0 + rc, sl] = a
                one_minus_a2 = -jnp.tanh(log_a) * (a * a + 1.0)
                b_ref[d, r0:r0 + rc, sl] = jnp.sqrt(one_minus_a2) * (ig * cv[:, sl])

    for d in range(2):
        reverse = d == 1
        h0 = jnp.broadcast_to(h0_ref[0, d:d + 1, :], (SUBLANES, BRANCH_W))
        if not reverse:
            def emit(r, hh):
                hs_ref[pl.ds(r, SUBLANES), :] = hh
        else:
            def emit(r, hh):
                tot = hs_ref[pl.ds(r, SUBLANES), :] + hh
                y_ref[pl.ds(r, SUBLANES), :] = (tot * jax.nn.gelu(lg_ref[pl.ds(r, SUBLANES), :])).astype(BF16)
        h_end = _lru_scan(a_ref.at[d], b_ref.at[d], h0, nblk, reverse, emit)
        hl_ref[0, d:d + 1, :] = h_end[0:1, :]


def _lru(p, lru_col, gate, L, layer, conv_w, conv_b, w_rg, b_rg, w_ig, b_ig, lam, h0):
    nb = p.shape[0] // L
    gate_w = _layer_spec(layer, 2, N_GROUPS, GROUP_W, GROUP_W)
    dir_vec = _layer_spec(layer, 2, 1, BRANCH_W)
    return pl.pallas_call(
        functools.partial(_lru_body, L=L),
        grid=(nb,),
        in_specs=[
            pl.BlockSpec((L, BRANCH_W), lambda b: (b, lru_col)),
            pl.BlockSpec((L, BRANCH_W), lambda b: (b, 0)),
            _layer_spec(layer, CONV_W, BRANCH_W),
            _layer_spec(layer, 1, BRANCH_W),
            gate_w, dir_vec, gate_w, dir_vec, dir_vec,
            pl.BlockSpec((1, 2, BRANCH_W), lambda b: (b, 0, 0)),
        ],
        out_specs=[
            pl.BlockSpec((L, BRANCH_W), lambda b: (b, 0)),
            pl.BlockSpec((1, 2, BRANCH_W), lambda b: (b, 0, 0)),
        ],
        out_shape=[
            jax.ShapeDtypeStruct((p.shape[0], BRANCH_W), BF16),
            jax.ShapeDtypeStruct((nb, 2, BRANCH_W), F32),
        ],
        scratch_shapes=[
            pltpu.VMEM((L + 2 * CONV_HALO, BRANCH_W), F32),
            pltpu.VMEM((2, L, BRANCH_W), F32),
            pltpu.VMEM((2, L, BRANCH_W), F32),
            pltpu.VMEM((L, BRANCH_W), F32),
        ],
        compiler_params=_params("parallel"),
        name="lru",
    )(p, gate, conv_w, conv_b, w_rg, b_rg, w_ig, b_ig, lam, h0)


_NT = (((1,), (1,)), ((), ()))


def _softmax_pv(scores, values):
    m = scores[0].max(axis=-1, keepdims=True)
    for s in scores[1:]:
        m = jnp.maximum(m, s.max(axis=-1, keepdims=True))
    es = [jnp.exp2(s - m) for s in scores]
    tot = es[0].sum(axis=-1, keepdims=True)
    for e in es[1:]:
        tot = tot + e.sum(axis=-1, keepdims=True)
    out = None
    for e, v in zip(es, values):
        o = jnp.dot(e.astype(BF16), v, preferred_element_type=F32)
        out = o if out is None else out + o
    return out * (1.0 / tot)


def _nattn_body(q_ref, k_ref, v_ref, kc_ref, vc_ref, t_ref, o_ref):
    rows = SEQ // GRID_W
    nq = NA_GROUP * GRID_W
    nk = NA_KEY_ROWS * GRID_W
    scale = GROUP_W ** -0.5 * LOG2_E
    kc = kc_ref[...]
    vc = vc_ref[...]

    def in_window(kr, row0):
        return ((kr >= row0) & (kr < row0 + NA_WIN_R)).astype(jnp.int32)

    def body(gi, carry):
        r0 = gi * NA_GROUP
        base = jnp.clip(r0 - NA_WIN_R // 2, 0, rows - NA_KEY_ROWS)
        qs = pl.multiple_of(r0 * GRID_W, nq)
        ks = pl.multiple_of(base * GRID_W, GRID_W)
        q = q_ref[pl.ds(qs, nq), :]
        kw = k_ref[pl.ds(ks, nk), :]
        vw = v_ref[pl.ds(ks, nk), :]
        s = lax.dot_general(q, kw, _NT, preferred_element_type=F32) * scale
        bias_rows = []
        for g in range(NA_GROUP):
            r = r0 + g
            row0 = jnp.clip(r - NA_WIN_R // 2, 0, rows - NA_WIN_R)
            pieces = []
            for jj in range(NA_KEY_ROWS // 2):
                kr = base + 2 * jj
                variant = in_window(kr, row0) + 2 * in_window(kr + 1, row0)
                e = jnp.clip(kr - r + NA_WIN_R, 0, NA_D - 1)
                pieces.append(t_ref[variant * NA_D + e])
            bias_rows.append(jnp.concatenate(pieces, axis=1))
        bias = jnp.concatenate(bias_rows, axis=0)
        s = jnp.where(bias > 0.5 * NEG_INF, s + bias, NEG_INF)
        sc = lax.dot_general(q, kc, _NT, preferred_element_type=F32) * scale
        o_ref[pl.ds(qs, nq), :] = _softmax_pv([s, sc], [vw, vc]).astype(BF16)
        return carry

    lax.fori_loop(0, rows // NA_GROUP, body, 0, unroll=2)


def _nattn(qkv, qkv_c, layer, bias_table):
    return pl.pallas_call(
        _nattn_body,
        grid=(BATCH, N_GROUPS),
        in_specs=[
            pl.BlockSpec((SEQ, GROUP_W), lambda b, h: (b, Q_COL + h)),
            pl.BlockSpec((SEQ, GROUP_W), lambda b, h: (b, K_COL + h)),
            pl.BlockSpec((SEQ, GROUP_W), lambda b, h: (b, V_COL + h)),
            pl.BlockSpec((CTX_LEN, GROUP_W), lambda b, h: (b, K_COL + h)),
            pl.BlockSpec((CTX_LEN, GROUP_W), lambda b, h: (b, V_COL + h)),
            pl.BlockSpec((None, None, 4 * NA_D, GRID_W, 2 * GRID_W), lambda b, h: (layer, h, 0, 0, 0)),
        ],
        out_specs=pl.BlockSpec((SEQ, GROUP_W), lambda b, h: (b, h)),
        out_shape=jax.ShapeDtypeStruct((qkv.shape[0], BRANCH_W), BF16),
        compiler_params=_params("parallel", "parallel"),
        name="nattn",
    )(qkv, qkv, qkv, qkv_c, qkv_c, bias_table)


def _cattn_body(q_ref, k_ref, v_ref, o_ref):
    scale = GROUP_W ** -0.5 * LOG2_E
    s = lax.dot_general(q_ref[...], k_ref[...], _NT, preferred_element_type=F32) * scale
    o_ref[...] = _softmax_pv([s], [v_ref[...]]).astype(BF16)


def _cattn(qkv_c):
    return pl.pallas_call(
        _cattn_body,
        grid=(BATCH, N_GROUPS),
        in_specs=[
            pl.BlockSpec((CTX_LEN, GROUP_W), lambda b, h: (b, Q_COL + h)),
            pl.BlockSpec((CTX_LEN, GROUP_W), lambda b, h: (b, K_COL + h)),
            pl.BlockSpec((CTX_LEN, GROUP_W), lambda b, h: (b, V_COL + h)),
        ],
        out_specs=pl.BlockSpec((CTX_LEN, GROUP_W), lambda b, h: (b, h)),
        out_shape=jax.ShapeDtypeStruct((qkv_c.shape[0], BRANCH_W), BF16),
        compiler_params=_params("parallel", "parallel"),
        name="cattn",
    )(qkv_c, qkv_c, qkv_c)


def _bias_table(rpb):
    col = jnp.arange(GRID_W)
    col_start = jnp.clip(col - NA_WIN_C // 2, 0, GRID_W - NA_WIN_C)
    col_ok = (col[None, :] >= col_start[:, None]) & (col[None, :] < col_start[:, None] + NA_WIN_C)
    pad = GRID_W - NA_WIN_C
    period = 2 * GRID_W - 1
    u = jnp.pad(rpb.astype(F32), [(0, 0)] * 3 + [(pad, pad)])
    skew = jnp.tile(u, GRID_W + 1)[..., :GRID_W * (period + 1)]
    skew = skew.reshape(rpb.shape[:3] + (GRID_W, period + 1))
    toe = skew[..., ::-1, :GRID_W]
    tab = jnp.where(col_ok, toe * LOG2_E, NEG_INF)
    off = jnp.full_like(tab[:, :, :1], NEG_INF)
    lo = jnp.concatenate([off, tab], axis=2)
    hi = jnp.concatenate([tab, off], axis=2)
    none = jnp.full_like(lo, NEG_INF)
    variants = [jnp.concatenate([lo if v & 1 else none, hi if v & 2 else none], axis=-1) for v in range(4)]
    return jnp.concatenate(variants, axis=2)


def _merge_body(h_ref, y0_ref, y1_ref, y2_ref, y3_ref, g0_ref, g1_ref, g2_ref, g3_ref, wb_ref, o_ref):
    h = h_ref[...]
    acc = None
    for n, (y_ref, g_ref) in enumerate(((y0_ref, g0_ref), (y1_ref, g1_ref), (y2_ref, g2_ref), (y3_ref, g3_ref))):
        gate = jax.nn.sigmoid(jnp.dot(h, g_ref[...], preferred_element_type=F32))
        term = gate * jnp.dot(y_ref[...], wb_ref[n], preferred_element_type=F32)
        acc = term if acc is None else acc + term
    o_ref[...] = acc.astype(BF16)


def _merge(h, ys, layer, w_gate, w_branch):
    rows = h.shape[0]
    tm, tn = TM_MERGE, 512
    nj = D_MODEL // tn
    gate_spec = lambda n: pl.BlockSpec((D_MODEL, tn), lambda i, j: (0, n * nj + j))
    y_spec = pl.BlockSpec((tm, BRANCH_W), lambda i, j: (i, 0))
    return pl.pallas_call(
        _merge_body,
        grid=(rows // tm, nj),
        in_specs=[pl.BlockSpec((tm, D_MODEL), lambda i, j: (i, 0)), y_spec, y_spec, y_spec, y_spec,
                  gate_spec(0), gate_spec(1), gate_spec(2), gate_spec(3),
                  pl.BlockSpec((None, N_BRANCH, BRANCH_W, tn), lambda i, j: (layer, 0, 0, j))],
        out_specs=pl.BlockSpec((tm, tn), lambda i, j: (i, j)),
        out_shape=jax.ShapeDtypeStruct((rows, D_MODEL), BF16),
        compiler_params=_params("parallel", "arbitrary"),
        name="merge",
    )(h, *ys, w_gate, w_gate, w_gate, w_gate, w_branch)


def _outproj_body(m_ref, x_ref, w_ref, gt_ref, g_ref, sh_ref, sc_ref, xm_ref, h2_ref, gs_ref, *, tm):
    xm_ref[...] = x_ref[...] + gt_ref[...] * jnp.dot(m_ref[...], w_ref[...], preferred_element_type=F32)
    _rms_mod_rows(xm_ref, h2_ref, g_ref, sc_ref, sh_ref, gs_ref, tm)


def _outproj(merged, x, mods, layer, row_of_tile, w_out, g_ffn):
    rows = x.shape[0]
    tm = TM_OUT
    row_spec = pl.BlockSpec((tm, D_MODEL), lambda i: (i, 0))
    return pl.pallas_call(
        functools.partial(_outproj_body, tm=tm),
        grid=(rows // tm,),
        in_specs=[row_spec, row_spec,
                  _layer_spec(layer, D_MODEL, D_MODEL),
                  _mod_spec(layer, 2, row_of_tile(tm)),
                  _layer_spec(layer, 1, D_MODEL),
                  _mod_spec(layer, 3, row_of_tile(tm)),
                  _mod_spec(layer, 4, row_of_tile(tm))],
        out_specs=[row_spec, row_spec],
        out_shape=[jax.ShapeDtypeStruct((rows, D_MODEL), F32),
                   jax.ShapeDtypeStruct((rows, D_MODEL), BF16)],
        scratch_shapes=[pltpu.VMEM((1, D_MODEL), F32)],
        compiler_params=_params("parallel"),
        name="outproj",
    )(merged, x, w_out, mods, g_ffn, mods, mods)


def _ffn_body(h_ref, wg_ref, wu_ref, wd_ref, xm_ref, gt_ref, gf_ref, o_ref, a_ref, rs_ref, *, tm, tf, final):
    f = pl.program_id(1)

    @pl.when(f == 0)
    def _():
        o_ref[...] = jnp.zeros_like(o_ref)

    h = h_ref[...]
    for s in range(tf // FFN_SUB):
        sl = slice(s * FFN_SUB, (s + 1) * FFN_SUB)
        g = jnp.dot(h, wg_ref[:, sl], preferred_element_type=F32)
        u = jnp.dot(h, wu_ref[:, sl], preferred_element_type=F32)
        a_ref[:, sl] = (jax.nn.silu(g) * u).astype(BF16)
        o_ref[...] += jnp.dot(a_ref[:, sl], wd_ref[sl, :], preferred_element_type=F32)

    @pl.when(f == pl.num_programs(1) - 1)
    def _():
        def residual(c, carry):
            r = pl.multiple_of(c * NORM_ROWS, NORM_ROWS)
            y = xm_ref[pl.ds(r, NORM_ROWS), :] + gt_ref[...] * o_ref[pl.ds(r, NORM_ROWS), :]
            o_ref[pl.ds(r, NORM_ROWS), :] = y
            if final:
                ms = jnp.mean(y * y, axis=-1, keepdims=True)
                rs_ref[pl.ds(r, NORM_ROWS), :] = jnp.broadcast_to(lax.rsqrt(ms + EPS), (NORM_ROWS, LANES))
            return carry
        lax.fori_loop(0, tm // NORM_ROWS, residual, 0, unroll=4)

        if final:
            def scale(c, carry):
                r = pl.multiple_of(c * NORM_ROWS, NORM_ROWS)
                rs = jnp.concatenate([rs_ref[pl.ds(r, NORM_ROWS), :]] * (D_MODEL // LANES), axis=1)
                o_ref[pl.ds(r, NORM_ROWS), :] = (o_ref[pl.ds(r, NORM_ROWS), :] * rs) * gf_ref[...]
                return carry
            lax.fori_loop(0, tm // NORM_ROWS, scale, 0, unroll=4)


def _ffn(h2, xm, mods, layer, row_of_tile, w_g, w_u, w_d, g_final, final):
    rows = xm.shape[0]
    tm, tf = TM_FFN, 512
    row_spec = pl.BlockSpec((tm, D_MODEL), lambda i, f: (i, 0))
    return pl.pallas_call(
        functools.partial(_ffn_body, tm=tm, tf=tf, final=final),
        grid=(rows // tm, D_FF // tf),
        in_specs=[row_spec,
                  pl.BlockSpec((None, D_MODEL, tf), lambda i, f: (layer, 0, f)),
                  pl.BlockSpec((None, D_MODEL, tf), lambda i, f: (layer, 0, f)),
                  pl.BlockSpec((None, tf, D_MODEL), lambda i, f: (layer, f, 0)),
                  row_spec,
                  _mod_spec(layer, 5, row_of_tile(tm)),
                  pl.BlockSpec((1, D_MODEL), lambda i, f: (0, 0))],
        out_specs=row_spec,
        out_shape=jax.ShapeDtypeStruct((rows, D_MODEL), F32),
        scratch_shapes=[pltpu.VMEM((tm, tf), BF16), pltpu.VMEM((tm, LANES), F32)],
        compiler_params=_params("parallel", "arbitrary"),
        name="ffn",
    )(h2, w_g, w_u, w_d, xm, mods, g_final)


def kernel(x, c, ctx, c_ctx, w_mod, b_mod, g_mix, g_ffn, g_final, w_in, w_pool, pool_scale, gmlp_ln_g, w_sp, b_sp, conv_w, conv_b, w_rg, b_rg, w_ig, b_ig, lru_lam, rpb, w_branch, w_gate, w_out, w_ffn_gate, w_ffn_up, w_ffn_down):
    B, L, D = x.shape
    C = ctx.shape[1]
    assert (B, L, D, C) == (BATCH, SEQ, D_MODEL, CTX_LEN)
    xs = x.reshape(B * L, D)
    cs = ctx.reshape(B * C, D)

    cond = jnp.concatenate([c, c_ctx[None, :], jnp.zeros((COND_ROWS - B - 1, D), F32)], axis=0)
    mods = _adaln_all(cond, w_mod, b_mod)

    w_in_b = w_in.astype(BF16)
    w_gate_b = None
    w_branch_b = w_branch.astype(BF16)
    w_out_b = w_out.astype(BF16)
    w_fg, w_fu, w_fd = w_ffn_gate.astype(BF16), w_ffn_up.astype(BF16), w_ffn_down.astype(BF16)
    g_mix_r = g_mix.reshape(DEPTH, 1, D)
    g_ffn_r = g_ffn.reshape(DEPTH, 1, D)
    g_fin_r = g_final.reshape(1, D)
    pool_args = (w_pool.astype(BF16), pool_scale.reshape(DEPTH, 1, BRANCH_W))
    sgu_args = (gmlp_ln_g.reshape(DEPTH, 1, BRANCH_W), w_sp.astype(BF16), b_sp.reshape(DEPTH, N_GROUPS, CHUNK, 1))
    lru_args = (conv_w, conv_b.reshape(DEPTH, 1, BRANCH_W),
                w_rg.astype(BF16), b_rg.reshape(DEPTH, 2, 1, BRANCH_W),
                w_ig.astype(BF16), b_ig.reshape(DEPTH, 2, 1, BRANCH_W),
                lru_lam.reshape(DEPTH, 2, 1, BRANCH_W))
    bias_table = _bias_table(rpb)
    h_zero = jnp.zeros((B, 2, BRANCH_W), F32)

    for l in range(DEPTH):
        last = l == DEPTH - 1
        px, gx, qkv, hx = _inproj(xs, mods, l, _latent_row, g_mix_r, w_in_b)
        if w_gate_b is None:
            hx, w_gate = lax.optimization_barrier((hx, w_gate))
            w_gate_b = [w_gate[k].astype(BF16).reshape(D, N_BRANCH * D) for k in range(DEPTH)]
        ctx_first = 1 if last else 0
        pc, gc, qkv_c, hc = _inproj(cs, mods, l, _context_row, g_mix_r, w_in_b, ctx_first)

        yc_lru, hc_state = _lru(pc, S_LRU - 2 * ctx_first, gc, C, l, *lru_args, h_zero)

        y_pool = _pool(px, L, l, *pool_args)
        y_sgu = _sgu(px, L, l, *sgu_args)
        y_lru, _ = _lru(px, S_LRU, gx, L, l, *lru_args, hc_state)
        y_na = _nattn(qkv, qkv_c, l, bias_table)
        merged = _merge(hx, (y_pool, y_sgu, y_lru, y_na), l, w_gate_b[l], w_branch_b)
        xm, hx2 = _outproj(merged, xs, mods, l, _latent_row, w_out_b, g_ffn_r)

        if not last:
            yc_pool = _pool(pc, C, l, *pool_args)
            yc_sgu = _sgu(pc, C, l, *sgu_args)
            yc_na = _cattn(qkv_c)
            merged_c = _merge(hc, (yc_pool, yc_sgu, yc_lru, yc_na), l, w_gate_b[l], w_branch_b)
            cm, hc2 = _outproj(merged_c, cs, mods, l, _context_row, w_out_b, g_ffn_r)
            cs = _ffn(hc2, cm, mods, l, _context_row, w_fg, w_fu, w_fd, g_fin_r, False)

        xs = _ffn(hx2, xm, mods, l, _latent_row, w_fg, w_fu, w_fd, g_fin_r, last)

    return xs.reshape(B, L, D)
```

```python
import functools

import jax
import jax.numpy as jnp
from jax import lax
from jax.experimental import pallas as pl
from jax.experimental.pallas import tpu as pltpu

D_MODEL = 2048
BATCH = 8
SEQ = 2048
DEPTH = 2
CTX_LEN = 256
GRID_W = 64
N_BRANCH = 4
BRANCH_W = D_MODEL // N_BRANCH
GROUP_W = 128
N_GROUPS = BRANCH_W // GROUP_W
N_IN_SLICES = 8
IN_W = N_IN_SLICES * BRANCH_W
POOL_WINDOWS = (2, 4, 8, 16)
CHUNK = 128
CONV_W = 4
LRU_C = 8.0
NA_WIN_R = 8
NA_WIN_C = 16
D_FF = 5632
EPS = 1e-6
NEG_INF = -1e30
LOG2_E = 1.4426950408889634

N_MOD = 6
COND_ROWS = 16
CTX_COND_ROW = BATCH
SUBLANES = 8
LANES = 128
POOL_HALO = 16
CONV_HALO = 8
VMEM_LIMIT = 60 * 1024 * 1024

F32 = jnp.float32
BF16 = jnp.bfloat16

S_POOL, S_U, S_V, S_LRU, S_LG, S_Q, S_K, S_VAL = range(N_IN_SLICES)
N_F32_SLICES = S_Q
HEADS_PER_SLICE = BRANCH_W // GROUP_W
Q_COL, K_COL, V_COL = (HEADS_PER_SLICE * (s - N_F32_SLICES) for s in (S_Q, S_K, S_VAL))
NA_GROUP = 4
NA_KEY_ROWS = 12
NA_D = 2 * NA_WIN_R
FFN_SUB = 256
NORM_ROWS = 16
NORM_GROUP = 8

TM_IN, TM_MERGE, TM_OUT, TM_FFN = 1024, 1024, 512, 1024


def _params(*sem):
    return pltpu.CompilerParams(dimension_semantics=sem, vmem_limit_bytes=VMEM_LIMIT)


def _layer_spec(layer, *shape):
    zeros = (0,) * len(shape)
    return pl.BlockSpec((None,) + shape, lambda *_: (layer,) + zeros)


def _mod_spec(layer, which, row_of_tile):
    def index(i, *_):
        return ((layer * COND_ROWS + row_of_tile(i)) * N_MOD + which, 0, 0)
    return pl.BlockSpec((None, 1, D_MODEL), index)


def _latent_row(tm):
    return lambda i: i // (SEQ // tm)


def _context_row(tm):
    return lambda i: CTX_COND_ROW


def _rms_mod_rows(src_ref, dst_ref, g_ref, sc_ref, sh_ref, gs_ref, rows):
    gs_ref[...] = g_ref[...] * (1.0 + sc_ref[...])
    group = NORM_GROUP * NORM_ROWS

    def body(c, carry):
        base = pl.multiple_of(c * group, group)
        scales = []
        for k in range(NORM_GROUP):
            x = src_ref[pl.ds(base + k * NORM_ROWS, NORM_ROWS), :]
            scales.append(lax.rsqrt(jnp.mean(x * x, axis=-1, keepdims=True) + EPS))
        for k in range(NORM_GROUP):
            x = src_ref[pl.ds(base + k * NORM_ROWS, NORM_ROWS), :]
            y = (x * scales[k]) * gs_ref[...] + sh_ref[...]
            dst_ref[pl.ds(base + k * NORM_ROWS, NORM_ROWS), :] = y.astype(dst_ref.dtype)
        return carry
    lax.fori_loop(0, rows // group, body, 0)


def _mod_body(c_ref, w_ref, b_ref, o_ref):
    @pl.when(pl.program_id(1) == 0)
    def _():
        o_ref[...] = jnp.broadcast_to(b_ref[...], o_ref.shape)
    a = jax.nn.silu(c_ref[...]).astype(BF16)
    o_ref[...] += jnp.dot(a, w_ref[...].astype(BF16), preferred_element_type=F32)


def _adaln_all(cond, w_mod, b_mod):
    tk = 256
    n = N_MOD * D_MODEL
    out = pl.pallas_call(
        _mod_body,
        grid=(DEPTH, D_MODEL // tk),
        in_specs=[
            pl.BlockSpec((COND_ROWS, tk), lambda l, k: (0, k)),
            pl.BlockSpec((None, tk, n), lambda l, k: (l, k, 0)),
            pl.BlockSpec((None, 1, n), lambda l, k: (l, 0, 0)),
        ],
        out_specs=pl.BlockSpec((None, COND_ROWS, n), lambda l, k: (l, 0, 0)),
        out_shape=jax.ShapeDtypeStruct((DEPTH, COND_ROWS, n), F32),
        compiler_params=_params("parallel", "arbitrary"),
        name="adaln",
    )(cond, w_mod, b_mod.reshape(DEPTH, 1, n))
    return out.reshape(DEPTH * COND_ROWS * N_MOD, 1, D_MODEL)


def _inproj_body(x_ref, sh_ref, sc_ref, g_ref, w_ref, a_ref, b_ref, c_ref, h_ref, gs_ref, *, tm, first_pair):
    j = pl.program_id(1)
    pair = j + first_pair
    lo, hi = slice(0, BRANCH_W), slice(BRANCH_W, 2 * BRANCH_W)

    @pl.when(j == 0)
    def _():
        _rms_mod_rows(x_ref, h_ref, g_ref, sc_ref, sh_ref, gs_ref, tm)

    def project(cols):
        return jnp.dot(h_ref[...], w_ref[:, cols], preferred_element_type=F32)

    @pl.when(pair < 2)
    def _():
        a_ref[:, lo] = project(lo)
        a_ref[:, hi] = project(hi)

    @pl.when(pair == 2)
    def _():
        b_ref[...] = project(lo)
        c_ref[:, lo] = project(hi).astype(BF16)

    @pl.when(pair == 3)
    def _():
        c_ref[:, BRANCH_W:2 * BRANCH_W] = project(lo).astype(BF16)
        c_ref[:, 2 * BRANCH_W:3 * BRANCH_W] = project(hi).astype(BF16)


def _inproj(x, mods, layer, row_of_tile, g_mix, w_in, first_pair=0):
    rows = x.shape[0]
    tm, tn = TM_IN, 2 * BRANCH_W
    n_pairs = N_IN_SLICES // 2
    a_pairs = 2 - first_pair
    assert a_pairs >= 1
    return pl.pallas_call(
        functools.partial(_inproj_body, tm=tm, first_pair=first_pair),
        grid=(rows // tm, n_pairs - first_pair),
        in_specs=[
            pl.BlockSpec((tm, D_MODEL), lambda i, j: (i, 0)),
            _mod_spec(layer, 0, row_of_tile(tm)),
            _mod_spec(layer, 1, row_of_tile(tm)),
            _layer_spec(layer, 1, D_MODEL),
            pl.BlockSpec((None, D_MODEL, tn), lambda i, j: (layer, 0, j + first_pair)),
        ],
        out_specs=[
            pl.BlockSpec((tm, tn), lambda i, j: (i, jnp.minimum(j, a_pairs - 1))),
            pl.BlockSpec((tm, BRANCH_W), lambda i, j: (i, 0)),
            pl.BlockSpec((tm, 3 * BRANCH_W), lambda i, j: (i, 0)),
            pl.BlockSpec((tm, D_MODEL), lambda i, j: (i, 0)),
        ],
        out_shape=[
            jax.ShapeDtypeStruct((rows, a_pairs * tn), F32),
            jax.ShapeDtypeStruct((rows, BRANCH_W), F32),
            jax.ShapeDtypeStruct((rows, 3 * BRANCH_W), BF16),
            jax.ShapeDtypeStruct((rows, D_MODEL), BF16),
        ],
        scratch_shapes=[pltpu.VMEM((1, D_MODEL), F32)],
        compiler_params=_params("parallel", "arbitrary"),
        name="inproj",
    )(x, mods, mods, g_mix, w_in)


def _pool_body(z_ref, w_ref, s_ref, o_ref, zp_ref, *, L):
    halo = jnp.zeros((POOL_HALO, BRANCH_W), F32)
    zp_ref[0:POOL_HALO, :] = halo
    zp_ref[POOL_HALO + L:2 * POOL_HALO + L, :] = halo
    zp_ref[POOL_HALO:POOL_HALO + L, :] = z_ref[...]
    rc = CHUNK
    for c in range(L // rc):
        r0 = c * rc
        t = r0 + lax.broadcasted_iota(jnp.int32, (rc, GROUP_W), 0)
        for gi, win in enumerate(POOL_WINDOWS):
            half = win // 2
            sl = slice(gi * GROUP_W, (gi + 1) * GROUP_W)
            acc = zp_ref[POOL_HALO + r0 - half:POOL_HALO + r0 - half + rc, sl]
            for o in range(-half + 1, half):
                acc = acc + zp_ref[POOL_HALO + r0 + o:POOL_HALO + r0 + o + rc, sl]
            cnt = (jnp.minimum(t + half, L) - jnp.maximum(t - half, 0)).astype(F32)
            pooled = acc / cnt - z_ref[r0:r0 + rc, sl]
            y = jnp.dot(pooled.astype(BF16), w_ref[gi], preferred_element_type=F32)
            o_ref[r0:r0 + rc, sl] = (y * s_ref[:, sl]).astype(BF16)


def _pool(p, L, layer, w_pool, pool_scale):
    nb = p.shape[0] // L
    return pl.pallas_call(
        functools.partial(_pool_body, L=L),
        grid=(nb,),
        in_specs=[
            pl.BlockSpec((L, BRANCH_W), lambda b: (b, S_POOL)),
            _layer_spec(layer, N_GROUPS, GROUP_W, GROUP_W),
            _layer_spec(layer, 1, BRANCH_W),
        ],
        out_specs=pl.BlockSpec((L, BRANCH_W), lambda b: (b, 0)),
        out_shape=jax.ShapeDtypeStruct((p.shape[0], BRANCH_W), BF16),
        scratch_shapes=[pltpu.VMEM((L + 2 * POOL_HALO, BRANCH_W), F32)],
        compiler_params=_params("parallel"),
        name="pool",
    )(p, w_pool, pool_scale)


def _sgu_body(u_ref, v_ref, g_ref, w_ref, b_ref, o_ref, *, L):
    for n in range(L // CHUNK):
        r0 = n * CHUNK
        v = v_ref[r0:r0 + CHUNK, :]
        vc = v - jnp.mean(v, axis=-1, keepdims=True)
        var = jnp.mean(vc * vc, axis=-1, keepdims=True)
        vn = ((vc * lax.rsqrt(var + EPS)) * g_ref[...]).astype(BF16)
        for gi in range(N_GROUPS):
            sl = slice(gi * GROUP_W, (gi + 1) * GROUP_W)
            mixed = jnp.dot(w_ref[gi], vn[:, sl], preferred_element_type=F32) + b_ref[gi]
            o_ref[r0:r0 + CHUNK, sl] = (u_ref[r0:r0 + CHUNK, sl] * mixed).astype(BF16)


def _sgu(p, L, layer, ln_g, w_sp, b_sp):
    nb = p.shape[0] // L
    return pl.pallas_call(
        functools.partial(_sgu_body, L=L),
        grid=(nb,),
        in_specs=[
            pl.BlockSpec((L, BRANCH_W), lambda b: (b, S_U)),
            pl.BlockSpec((L, BRANCH_W), lambda b: (b, S_V)),
            _layer_spec(layer, 1, BRANCH_W),
            _layer_spec(layer, N_GROUPS, CHUNK, CHUNK),
            _layer_spec(layer, N_GROUPS, CHUNK, 1),
        ],
        out_specs=pl.BlockSpec((L, BRANCH_W), lambda b: (b, 0)),
        out_shape=jax.ShapeDtypeStruct((p.shape[0], BRANCH_W), BF16),
        compiler_params=_params("parallel"),
        name="sgu",
    )(p, p, ln_g, w_sp, b_sp)


def _lru_scan(a_ref, b_ref, h0, nblk, reverse, emit):
    row = lax.broadcasted_iota(jnp.int32, (SUBLANES, BRANCH_W), 0)

    def body(jj, h):
        j = (nblk - 1 - jj) if reverse else jj
        r = pl.multiple_of(j * SUBLANES, SUBLANES)
        a = a_ref[pl.ds(r, SUBLANES), :]
        b = b_ref[pl.ds(r, SUBLANES), :]
        for s in (1, 2, 4):
            shift = (SUBLANES - s) if reverse else s
            a_s = pltpu.roll(a, shift, 0)
            b_s = pltpu.roll(b, shift, 0)
            m = (row < SUBLANES - s) if reverse else (row >= s)
            b = jnp.where(m, a * b_s + b, b)
            a = jnp.where(m, a * a_s, a)
        hh = a * h + b
        emit(r, hh)
        edge = hh[0:1, :] if reverse else hh[SUBLANES - 1:SUBLANES, :]
        return jnp.broadcast_to(edge, (SUBLANES, BRANCH_W))

    return lax.fori_loop(0, nblk, body, h0, unroll=4)


def _lru_body(z_ref, lg_ref, cw_ref, cb_ref, wr_ref, br_ref, wi_ref, bi_ref, lam_ref, h0_ref,
              y_ref, hl_ref, zp_ref, a_ref, b_ref, hs_ref, *, L):
    halo = jnp.zeros((CONV_HALO, BRANCH_W), F32)
    zp_ref[0:CONV_HALO, :] = halo
    zp_ref[CONV_HALO + L:2 * CONV_HALO + L, :] = halo
    zp_ref[CONV_HALO:CONV_HALO + L, :] = z_ref[...]
    left = CONV_W // 2
    rc = CHUNK
    nblk = L // SUBLANES

    def conv_chunk(r0):
        out = zp_ref[CONV_HALO + r0 - left:CONV_HALO + r0 - left + rc, :] * cw_ref[0:1, :] + cb_ref[...]
        for k in range(1, CONV_W):
            lo = CONV_HALO + r0 - left + k
            out = out + zp_ref[lo:lo + rc, :] * cw_ref[k:k + 1, :]
        return out

    def sigmoid(t):
        return 0.5 * jnp.tanh(0.5 * t) + 0.5

    sp = [jax.nn.softplus(-lam_ref[d]) for d in range(2)]
    for c in range(L // rc):
        r0 = c * rc
        cv = conv_chunk(r0)
        cvb = cv.astype(BF16)
        for d in range(2):
            for gi in range(N_GROUPS):
                sl = slice(gi * GROUP_W, (gi + 1) * GROUP_W)
                rg = sigmoid(jnp.dot(cvb[:, sl], wr_ref[d, gi], preferred_element_type=F32) + br_ref[d][:, sl])
                ig = sigmoid(jnp.dot(cvb[:, sl], wi_ref[d, gi], preferred_element_type=F32) + bi_ref[d][:, sl])
                log_a = (-LRU_C * rg) * sp[d][:, sl]
                a = jnp.exp(log_a)
                a_ref[d, r0:r0 + rc, sl] = a
                one_minus_a2 = -jnp.tanh(log_a) * (a * a + 1.0)
                b_ref[d, r0:r0 + rc, sl] = jnp.sqrt(one_minus_a2) * (ig * cv[:, sl])

    for d in range(2):
        reverse = d == 1
        h0 = jnp.broadcast_to(h0_ref[0, d:d + 1, :], (SUBLANES, BRANCH_W))
        if not reverse:
            def emit(r, hh):
                hs_ref[pl.ds(r, SUBLANES), :] = hh
        else:
            def emit(r, hh):
                tot = hs_ref[pl.ds(r, SUBLANES), :] + hh
                y_ref[pl.ds(r, SUBLANES), :] = (tot * jax.nn.gelu(lg_ref[pl.ds(r, SUBLANES), :])).astype(BF16)
        h_end = _lru_scan(a_ref.at[d], b_ref.at[d], h0, nblk, reverse, emit)
        hl_ref[0, d:d + 1, :] = h_end[0:1, :]


def _lru(p, lru_col, gate, L, layer, conv_w, conv_b, w_rg, b_rg, w_ig, b_ig, lam, h0):
    nb = p.shape[0] // L
    gate_w = _layer_spec(layer, 2, N_GROUPS, GROUP_W, GROUP_W)
    dir_vec = _layer_spec(layer, 2, 1, BRANCH_W)
    return pl.pallas_call(
        functools.partial(_lru_body, L=L),
        grid=(nb,),
        in_specs=[
            pl.BlockSpec((L, BRANCH_W), lambda b: (b, lru_col)),
            pl.BlockSpec((L, BRANCH_W), lambda b: (b, 0)),
            _layer_spec(layer, CONV_W, BRANCH_W),
            _layer_spec(layer, 1, BRANCH_W),
            gate_w, dir_vec, gate_w, dir_vec, dir_vec,
            pl.BlockSpec((1, 2, BRANCH_W), lambda b: (b, 0, 0)),
        ],
        out_specs=[
            pl.BlockSpec((L, BRANCH_W), lambda b: (b, 0)),
            pl.BlockSpec((1, 2, BRANCH_W), lambda b: (b, 0, 0)),
        ],
        out_shape=[
            jax.ShapeDtypeStruct((p.shape[0], BRANCH_W), BF16),
            jax.ShapeDtypeStruct((nb, 2, BRANCH_W), F32),
        ],
        scratch_shapes=[
            pltpu.VMEM((L + 2 * CONV_HALO, BRANCH_W), F32),
            pltpu.VMEM((2, L, BRANCH_W), F32),
            pltpu.VMEM((2, L, BRANCH_W), F32),
            pltpu.VMEM((L, BRANCH_W), F32),
        ],
        compiler_params=_params("parallel"),
        name="lru",
    )(p, gate, conv_w, conv_b, w_rg, b_rg, w_ig, b_ig, lam, h0)


_NT = (((1,), (1,)), ((), ()))


def _softmax_pv(scores, values):
    m = scores[0].max(axis=-1, keepdims=True)
    for s in scores[1:]:
        m = jnp.maximum(m, s.max(axis=-1, keepdims=True))
    es = [jnp.exp2(s - m) for s in scores]
    tot = es[0].sum(axis=-1, keepdims=True)
    for e in es[1:]:
        tot = tot + e.sum(axis=-1, keepdims=True)
    out = None
    for e, v in zip(es, values):
        o = jnp.dot(e.astype(BF16), v, preferred_element_type=F32)
        out = o if out is None else out + o
    return out * (1.0 / tot)


def _nattn_body(q_ref, k_ref, v_ref, kc_ref, vc_ref, t_ref, o_ref):
    rows = SEQ // GRID_W
    nq = NA_GROUP * GRID_W
    nk = NA_KEY_ROWS * GRID_W
    scale = GROUP_W ** -0.5 * LOG2_E

    def in_window(kr, row0):
        return ((kr >= row0) & (kr < row0 + NA_WIN_R)).astype(jnp.int32)

    def head(h):
        hl = slice(h * GROUP_W, (h + 1) * GROUP_W)
        kc = kc_ref[:, hl]
        vc = vc_ref[:, hl]

        def body(gi, carry):
            r0 = gi * NA_GROUP
            base = jnp.clip(r0 - NA_WIN_R // 2, 0, rows - NA_KEY_ROWS)
            qs = pl.multiple_of(r0 * GRID_W, nq)
            ks = pl.multiple_of(base * GRID_W, GRID_W)
            q = q_ref[pl.ds(qs, nq), hl]
            kw = k_ref[pl.ds(ks, nk), hl]
            vw = v_ref[pl.ds(ks, nk), hl]
            s = lax.dot_general(q, kw, _NT, preferred_element_type=F32) * scale
            bias_rows = []
            for g in range(NA_GROUP):
                r = r0 + g
                row0 = jnp.clip(r - NA_WIN_R // 2, 0, rows - NA_WIN_R)
                pieces = []
                for jj in range(NA_KEY_ROWS // 2):
                    kr = base + 2 * jj
                    variant = in_window(kr, row0) + 2 * in_window(kr + 1, row0)
                    e = jnp.clip(kr - r + NA_WIN_R, 0, NA_D - 1)
                    pieces.append(t_ref[h, variant * NA_D + e])
                bias_rows.append(jnp.concatenate(pieces, axis=1))
            bias = jnp.concatenate(bias_rows, axis=0)
            s = jnp.where(bias > 0.5 * NEG_INF, s + bias, NEG_INF)
            sc = lax.dot_general(q, kc, _NT, preferred_element_type=F32) * scale
            o_ref[pl.ds(qs, nq), hl] = _softmax_pv([s, sc], [vw, vc]).astype(BF16)
            return carry

        lax.fori_loop(0, rows // NA_GROUP, body, 0, unroll=2)

    for h in range(N_GROUPS):
        head(h)


def _nattn(qkv, qkv_c, layer, bias_table):
    slice_block = lambda rows, col: pl.BlockSpec((rows, BRANCH_W), lambda b: (b, col // HEADS_PER_SLICE))
    return pl.pallas_call(
        _nattn_body,
        grid=(BATCH,),
        in_specs=[
            slice_block(SEQ, Q_COL), slice_block(SEQ, K_COL), slice_block(SEQ, V_COL),
            slice_block(CTX_LEN, K_COL), slice_block(CTX_LEN, V_COL),
            _layer_spec(layer, N_GROUPS, 4 * NA_D, GRID_W, 2 * GRID_W),
        ],
        out_specs=pl.BlockSpec((SEQ, BRANCH_W), lambda b: (b, 0)),
        out_shape=jax.ShapeDtypeStruct((qkv.shape[0], BRANCH_W), BF16),
        compiler_params=_params("parallel"),
        name="nattn",
    )(qkv, qkv, qkv, qkv_c, qkv_c, bias_table)


def _cattn_body(q_ref, k_ref, v_ref, o_ref):
    scale = GROUP_W ** -0.5 * LOG2_E
    s = lax.dot_general(q_ref[...], k_ref[...], _NT, preferred_element_type=F32) * scale
    o_ref[...] = _softmax_pv([s], [v_ref[...]]).astype(BF16)


def _cattn(qkv_c):
    return pl.pallas_call(
        _cattn_body,
        grid=(BATCH, N_GROUPS),
        in_specs=[
            pl.BlockSpec((CTX_LEN, GROUP_W), lambda b, h: (b, Q_COL + h)),
            pl.BlockSpec((CTX_LEN, GROUP_W), lambda b, h: (b, K_COL + h)),
            pl.BlockSpec((CTX_LEN, GROUP_W), lambda b, h: (b, V_COL + h)),
        ],
        out_specs=pl.BlockSpec((CTX_LEN, GROUP_W), lambda b, h: (b, h)),
        out_shape=jax.ShapeDtypeStruct((qkv_c.shape[0], BRANCH_W), BF16),
        compiler_params=_params("parallel", "parallel"),
        name="cattn",
    )(qkv_c, qkv_c, qkv_c)


def _bias_table(rpb):
    col = jnp.arange(GRID_W)
    col_start = jnp.clip(col - NA_WIN_C // 2, 0, GRID_W - NA_WIN_C)
    col_ok = (col[None, :] >= col_start[:, None]) & (col[None, :] < col_start[:, None] + NA_WIN_C)
    pad = GRID_W - NA_WIN_C
    period = 2 * GRID_W - 1
    u = jnp.pad(rpb.astype(F32), [(0, 0)] * 3 + [(pad, pad)])
    skew = jnp.tile(u, GRID_W + 1)[..., :GRID_W * (period + 1)]
    skew = skew.reshape(rpb.shape[:3] + (GRID_W, period + 1))
    toe = skew[..., ::-1, :GRID_W]
    tab = jnp.where(col_ok, toe * LOG2_E, NEG_INF)
    off = jnp.full_like(tab[:, :, :1], NEG_INF)
    lo = jnp.concatenate([off, tab], axis=2)
    hi = jnp.concatenate([tab, off], axis=2)
    none = jnp.full_like(lo, NEG_INF)
    variants = [jnp.concatenate([lo if v & 1 else none, hi if v & 2 else none], axis=-1) for v in range(4)]
    return jnp.concatenate(variants, axis=2)


def _merge_body(h_ref, y0_ref, y1_ref, y2_ref, y3_ref, g0_ref, g1_ref, g2_ref, g3_ref, wb_ref, o_ref):
    h = h_ref[...]
    acc = None
    for n, (y_ref, g_ref) in enumerate(((y0_ref, g0_ref), (y1_ref, g1_ref), (y2_ref, g2_ref), (y3_ref, g3_ref))):
        gate = jax.nn.sigmoid(jnp.dot(h, g_ref[...], preferred_element_type=F32))
        term = gate * jnp.dot(y_ref[...], wb_ref[n], preferred_element_type=F32)
        acc = term if acc is None else acc + term
    o_ref[...] = acc.astype(BF16)


def _merge(h, ys, layer, w_gate, w_branch):
    rows = h.shape[0]
    tm, tn = TM_MERGE, 512
    nj = D_MODEL // tn
    gate_spec = lambda n: pl.BlockSpec((D_MODEL, tn), lambda i, j: (0, n * nj + j))
    y_spec = pl.BlockSpec((tm, BRANCH_W), lambda i, j: (i, 0))
    return pl.pallas_call(
        _merge_body,
        grid=(rows // tm, nj),
        in_specs=[pl.BlockSpec((tm, D_MODEL), lambda i, j: (i, 0)), y_spec, y_spec, y_spec, y_spec,
                  gate_spec(0), gate_spec(1), gate_spec(2), gate_spec(3),
                  pl.BlockSpec((None, N_BRANCH, BRANCH_W, tn), lambda i, j: (layer, 0, 0, j))],
        out_specs=pl.BlockSpec((tm, tn), lambda i, j: (i, j)),
        out_shape=jax.ShapeDtypeStruct((rows, D_MODEL), BF16),
        compiler_params=_params("parallel", "arbitrary"),
        name="merge",
    )(h, *ys, w_gate, w_gate, w_gate, w_gate, w_branch)


def _outproj_body(m_ref, x_ref, w_ref, gt_ref, g_ref, sh_ref, sc_ref, xm_ref, h2_ref, gs_ref, *, tm):
    xm_ref[...] = x_ref[...] + gt_ref[...] * jnp.dot(m_ref[...], w_ref[...], preferred_element_type=F32)
    _rms_mod_rows(xm_ref, h2_ref, g_ref, sc_ref, sh_ref, gs_ref, tm)


def _outproj(merged, x, mods, layer, row_of_tile, w_out, g_ffn):
    rows = x.shape[0]
    tm = TM_OUT
    row_spec = pl.BlockSpec((tm, D_MODEL), lambda i: (i, 0))
    return pl.pallas_call(
        functools.partial(_outproj_body, tm=tm),
        grid=(rows // tm,),
        in_specs=[row_spec, row_spec,
                  _layer_spec(layer, D_MODEL, D_MODEL),
                  _mod_spec(layer, 2, row_of_tile(tm)),
                  _layer_spec(layer, 1, D_MODEL),
                  _mod_spec(layer, 3, row_of_tile(tm)),
                  _mod_spec(layer, 4, row_of_tile(tm))],
        out_specs=[row_spec, row_spec],
        out_shape=[jax.ShapeDtypeStruct((rows, D_MODEL), F32),
                   jax.ShapeDtypeStruct((rows, D_MODEL), BF16)],
        scratch_shapes=[pltpu.VMEM((1, D_MODEL), F32)],
        compiler_params=_params("parallel"),
        name="outproj",
    )(merged, x, w_out, mods, g_ffn, mods, mods)


def _ffn_body(h_ref, wg_ref, wu_ref, wd_ref, xm_ref, gt_ref, gf_ref, o_ref, a_ref, rs_ref, *, tm, tf, final):
    f = pl.program_id(1)

    @pl.when(f == 0)
    def _():
        o_ref[...] = jnp.zeros_like(o_ref)

    h = h_ref[...]
    for s in range(tf // FFN_SUB):
        sl = slice(s * FFN_SUB, (s + 1) * FFN_SUB)
        g = jnp.dot(h, wg_ref[:, sl], preferred_element_type=F32)
        u = jnp.dot(h, wu_ref[:, sl], preferred_element_type=F32)
        a_ref[:, sl] = (jax.nn.silu(g) * u).astype(BF16)
        o_ref[...] += jnp.dot(a_ref[:, sl], wd_ref[sl, :], preferred_element_type=F32)

    @pl.when(f == pl.num_programs(1) - 1)
    def _():
        def residual(c, carry):
            r = pl.multiple_of(c * NORM_ROWS, NORM_ROWS)
            y = xm_ref[pl.ds(r, NORM_ROWS), :] + gt_ref[...] * o_ref[pl.ds(r, NORM_ROWS), :]
            o_ref[pl.ds(r, NORM_ROWS), :] = y
            if final:
                ms = jnp.mean(y * y, axis=-1, keepdims=True)
                rs_ref[pl.ds(r, NORM_ROWS), :] = jnp.broadcast_to(lax.rsqrt(ms + EPS), (NORM_ROWS, LANES))
            return carry
        lax.fori_loop(0, tm // NORM_ROWS, residual, 0, unroll=4)

        if final:
            def scale(c, carry):
                r = pl.multiple_of(c * NORM_ROWS, NORM_ROWS)
                rs = jnp.concatenate([rs_ref[pl.ds(r, NORM_ROWS), :]] * (D_MODEL // LANES), axis=1)
                o_ref[pl.ds(r, NORM_ROWS), :] = (o_ref[pl.ds(r, NORM_ROWS), :] * rs) * gf_ref[...]
                return carry
            lax.fori_loop(0, tm // NORM_ROWS, scale, 0, unroll=4)


def _ffn(h2, xm, mods, layer, row_of_tile, w_g, w_u, w_d, g_final, final):
    rows = xm.shape[0]
    tm, tf = TM_FFN, 512
    row_spec = pl.BlockSpec((tm, D_MODEL), lambda i, f: (i, 0))
    return pl.pallas_call(
        functools.partial(_ffn_body, tm=tm, tf=tf, final=final),
        grid=(rows // tm, D_FF // tf),
        in_specs=[row_spec,
                  pl.BlockSpec((None, D_MODEL, tf), lambda i, f: (layer, 0, f)),
                  pl.BlockSpec((None, D_MODEL, tf), lambda i, f: (layer, 0, f)),
                  pl.BlockSpec((None, tf, D_MODEL), lambda i, f: (layer, f, 0)),
                  row_spec,
                  _mod_spec(layer, 5, row_of_tile(tm)),
                  pl.BlockSpec((1, D_MODEL), lambda i, f: (0, 0))],
        out_specs=row_spec,
        out_shape=jax.ShapeDtypeStruct((rows, D_MODEL), F32),
        scratch_shapes=[pltpu.VMEM((tm, tf), BF16), pltpu.VMEM((tm, LANES), F32)],
        compiler_params=_params("parallel", "arbitrary"),
        name="ffn",
    )(h2, w_g, w_u, w_d, xm, mods, g_final)


def kernel(x, c, ctx, c_ctx, w_mod, b_mod, g_mix, g_ffn, g_final, w_in, w_pool, pool_scale, gmlp_ln_g, w_sp, b_sp, conv_w, conv_b, w_rg, b_rg, w_ig, b_ig, lru_lam, rpb, w_branch, w_gate, w_out, w_ffn_gate, w_ffn_up, w_ffn_down):
    B, L, D = x.shape
    C = ctx.shape[1]
    assert (B, L, D, C) == (BATCH, SEQ, D_MODEL, CTX_LEN)
    xs = x.reshape(B * L, D)
    cs = ctx.reshape(B * C, D)

    cond = jnp.concatenate([c, c_ctx[None, :], jnp.zeros((COND_ROWS - B - 1, D), F32)], axis=0)
    mods = _adaln_all(cond, w_mod, b_mod)

    w_in_b = w_in.astype(BF16)
    w_gate_b = None
    w_branch_b = w_branch.astype(BF16)
    w_out_b = w_out.astype(BF16)
    w_fg, w_fu, w_fd = w_ffn_gate.astype(BF16), w_ffn_up.astype(BF16), w_ffn_down.astype(BF16)
    g_mix_r = g_mix.reshape(DEPTH, 1, D)
    g_ffn_r = g_ffn.reshape(DEPTH, 1, D)
    g_fin_r = g_final.reshape(1, D)
    pool_args = (w_pool.astype(BF16), pool_scale.reshape(DEPTH, 1, BRANCH_W))
    sgu_args = (gmlp_ln_g.reshape(DEPTH, 1, BRANCH_W), w_sp.astype(BF16), b_sp.reshape(DEPTH, N_GROUPS, CHUNK, 1))
    lru_args = (conv_w, conv_b.reshape(DEPTH, 1, BRANCH_W),
                w_rg.astype(BF16), b_rg.reshape(DEPTH, 2, 1, BRANCH_W),
                w_ig.astype(BF16), b_ig.reshape(DEPTH, 2, 1, BRANCH_W),
                lru_lam.reshape(DEPTH, 2, 1, BRANCH_W))
    bias_table = _bias_table(rpb)
    h_zero = jnp.zeros((B, 2, BRANCH_W), F32)

    for l in range(DEPTH):
        last = l == DEPTH - 1
        px, gx, qkv, hx = _inproj(xs, mods, l, _latent_row, g_mix_r, w_in_b)
        if w_gate_b is None:
            hx, w_gate = lax.optimization_barrier((hx, w_gate))
            w_gate_b = [w_gate[k].astype(BF16).reshape(D, N_BRANCH * D) for k in range(DEPTH)]
        ctx_first = 1 if last else 0
        pc, gc, qkv_c, hc = _inproj(cs, mods, l, _context_row, g_mix_r, w_in_b, ctx_first)

        yc_lru, hc_state = _lru(pc, S_LRU - 2 * ctx_first, gc, C, l, *lru_args, h_zero)

        y_pool = _pool(px, L, l, *pool_args)
        y_sgu = _sgu(px, L, l, *sgu_args)
        y_lru, _ = _lru(px, S_LRU, gx, L, l, *lru_args, hc_state)
        y_na = _nattn(qkv, qkv_c, l, bias_table)
        merged = _merge(hx, (y_pool, y_sgu, y_lru, y_na), l, w_gate_b[l], w_branch_b)
        xm, hx2 = _outproj(merged, xs, mods, l, _latent_row, w_out_b, g_ffn_r)

        if not last:
            yc_pool = _pool(pc, C, l, *pool_args)
            yc_sgu = _sgu(pc, C, l, *sgu_args)
            yc_na = _cattn(qkv_c)
            merged_c = _merge(hc, (yc_pool, yc_sgu, yc_lru, yc_na), l, w_gate_b[l], w_branch_b)
            cm, hc2 = _outproj(merged_c, cs, mods, l, _context_row, w_out_b, g_ffn_r)
            cs = _ffn(hc2, cm, mods, l, _context_row, w_fg, w_fu, w_fd, g_fin_r, False)

        xs = _ffn(hx2, xm, mods, l, _latent_row, w_fg, w_fu, w_fd, g_fin_r, last)

    return xs.reshape(B, L, D)
```

```python
import functools

import jax
import jax.numpy as jnp
from jax import lax
from jax.experimental import pallas as pl
from jax.experimental.pallas import tpu as pltpu

D_MODEL = 2048
BATCH = 8
SEQ = 2048
DEPTH = 2
CTX_LEN = 256
GRID_W = 64
N_BRANCH = 4
BRANCH_W = D_MODEL // N_BRANCH
GROUP_W = 128
N_GROUPS = BRANCH_W // GROUP_W
N_IN_SLICES = 8
IN_W = N_IN_SLICES * BRANCH_W
POOL_WINDOWS = (2, 4, 8, 16)
CHUNK = 128
CONV_W = 4
LRU_C = 8.0
NA_WIN_R = 8
NA_WIN_C = 16
D_FF = 5632
EPS = 1e-6
NEG_INF = -1e30
LOG2_E = 1.4426950408889634

N_MOD = 6
COND_ROWS = 16
CTX_COND_ROW = BATCH
SUBLANES = 8
LANES = 128
POOL_HALO = 16
CONV_HALO = 8
VMEM_LIMIT = 60 * 1024 * 1024

F32 = jnp.float32
BF16 = jnp.bfloat16

S_POOL, S_U, S_V, S_LRU, S_LG, S_Q, S_K, S_VAL = range(N_IN_SLICES)
N_F32_SLICES = S_Q
HEADS_PER_SLICE = BRANCH_W // GROUP_W
Q_COL, K_COL, V_COL = (HEADS_PER_SLICE * (s - N_F32_SLICES) for s in (S_Q, S_K, S_VAL))
NA_GROUP = 4
NA_KEY_ROWS = 12
NA_D = 2 * NA_WIN_R
FFN_SUB = 256
NORM_ROWS = 16
NORM_GROUP = 8

TM_IN, TM_MERGE, TM_OUT, TM_FFN = 1024, 1024, 512, 1024


def _params(*sem):
    return pltpu.CompilerParams(dimension_semantics=sem, vmem_limit_bytes=VMEM_LIMIT)


def _layer_spec(layer, *shape):
    zeros = (0,) * len(shape)
    return pl.BlockSpec((None,) + shape, lambda *_: (layer,) + zeros)


def _mod_spec(layer, which, row_of_tile):
    def index(i, *_):
        return ((layer * COND_ROWS + row_of_tile(i)) * N_MOD + which, 0, 0)
    return pl.BlockSpec((None, 1, D_MODEL), index)


def _latent_row(tm):
    return lambda i: i // (SEQ // tm)


def _context_row(tm):
    return lambda i: CTX_COND_ROW


def _rms_mod_rows(src_ref, dst_ref, g_ref, sc_ref, sh_ref, gs_ref, rows):
    gs_ref[...] = g_ref[...] * (1.0 + sc_ref[...])
    group = NORM_GROUP * NORM_ROWS

    def body(c, carry):
        base = pl.multiple_of(c * group, group)
        scales = []
        for k in range(NORM_GROUP):
            x = src_ref[pl.ds(base + k * NORM_ROWS, NORM_ROWS), :]
            scales.append(lax.rsqrt(jnp.mean(x * x, axis=-1, keepdims=True) + EPS))
        for k in range(NORM_GROUP):
            x = src_ref[pl.ds(base + k * NORM_ROWS, NORM_ROWS), :]
            y = (x * scales[k]) * gs_ref[...] + sh_ref[...]
            dst_ref[pl.ds(base + k * NORM_ROWS, NORM_ROWS), :] = y.astype(dst_ref.dtype)
        return carry
    lax.fori_loop(0, rows // group, body, 0)


def _mod_body(c_ref, w_ref, b_ref, o_ref):
    @pl.when(pl.program_id(1) == 0)
    def _():
        o_ref[...] = jnp.broadcast_to(b_ref[...], o_ref.shape)
    a = jax.nn.silu(c_ref[...]).astype(BF16)
    o_ref[...] += jnp.dot(a, w_ref[...].astype(BF16), preferred_element_type=F32)


def _adaln_all(cond, w_mod, b_mod):
    tk = 256
    n = N_MOD * D_MODEL
    out = pl.pallas_call(
        _mod_body,
        grid=(DEPTH, D_MODEL // tk),
        in_specs=[
            pl.BlockSpec((COND_ROWS, tk), lambda l, k: (0, k)),
            pl.BlockSpec((None, tk, n), lambda l, k: (l, k, 0)),
            pl.BlockSpec((None, 1, n), lambda l, k: (l, 0, 0)),
        ],
        out_specs=pl.BlockSpec((None, COND_ROWS, n), lambda l, k: (l, 0, 0)),
        out_shape=jax.ShapeDtypeStruct((DEPTH, COND_ROWS, n), F32),
        compiler_params=_params("parallel", "arbitrary"),
        name="adaln",
    )(cond, w_mod, b_mod.reshape(DEPTH, 1, n))
    return out.reshape(DEPTH * COND_ROWS * N_MOD, 1, D_MODEL)


def _inproj_body(x_ref, sh_ref, sc_ref, g_ref, w_ref, a_ref, b_ref, c_ref, h_ref, gs_ref, *, tm, first_pair):
    j = pl.program_id(1)
    pair = j + first_pair
    lo, hi = slice(0, BRANCH_W), slice(BRANCH_W, 2 * BRANCH_W)

    @pl.when(j == 0)
    def _():
        _rms_mod_rows(x_ref, h_ref, g_ref, sc_ref, sh_ref, gs_ref, tm)

    def project(cols):
        return jnp.dot(h_ref[...], w_ref[:, cols], preferred_element_type=F32)

    @pl.when(pair < 2)
    def _():
        a_ref[:, lo] = project(lo)
        a_ref[:, hi] = project(hi)

    @pl.when(pair == 2)
    def _():
        b_ref[...] = project(lo)
        c_ref[:, lo] = project(hi).astype(BF16)

    @pl.when(pair == 3)
    def _():
        c_ref[:, BRANCH_W:2 * BRANCH_W] = project(lo).astype(BF16)
        c_ref[:, 2 * BRANCH_W:3 * BRANCH_W] = project(hi).astype(BF16)


def _inproj(x, mods, layer, row_of_tile, g_mix, w_in, first_pair=0):
    rows = x.shape[0]
    tm, tn = TM_IN, 2 * BRANCH_W
    n_pairs = N_IN_SLICES // 2
    a_pairs = 2 - first_pair
    assert a_pairs >= 1
    return pl.pallas_call(
        functools.partial(_inproj_body, tm=tm, first_pair=first_pair),
        grid=(rows // tm, n_pairs - first_pair),
        in_specs=[
            pl.BlockSpec((tm, D_MODEL), lambda i, j: (i, 0)),
            _mod_spec(layer, 0, row_of_tile(tm)),
            _mod_spec(layer, 1, row_of_tile(tm)),
            _layer_spec(layer, 1, D_MODEL),
            pl.BlockSpec((None, D_MODEL, tn), lambda i, j: (layer, 0, j + first_pair)),
        ],
        out_specs=[
            pl.BlockSpec((tm, tn), lambda i, j: (i, jnp.minimum(j, a_pairs - 1))),
            pl.BlockSpec((tm, BRANCH_W), lambda i, j: (i, 0)),
            pl.BlockSpec((tm, 3 * BRANCH_W), lambda i, j: (i, 0)),
            pl.BlockSpec((tm, D_MODEL), lambda i, j: (i, 0)),
        ],
        out_shape=[
            jax.ShapeDtypeStruct((rows, a_pairs * tn), F32),
            jax.ShapeDtypeStruct((rows, BRANCH_W), F32),
            jax.ShapeDtypeStruct((rows, 3 * BRANCH_W), BF16),
            jax.ShapeDtypeStruct((rows, D_MODEL), BF16),
        ],
        scratch_shapes=[pltpu.VMEM((1, D_MODEL), F32)],
        compiler_params=_params("parallel", "arbitrary"),
        name="inproj",
    )(x, mods, mods, g_mix, w_in)


def _pool_body(z_ref, w_ref, s_ref, o_ref, zp_ref, *, L):
    halo = jnp.zeros((POOL_HALO, BRANCH_W), F32)
    zp_ref[0:POOL_HALO, :] = halo
    zp_ref[POOL_HALO + L:2 * POOL_HALO + L, :] = halo
    zp_ref[POOL_HALO:POOL_HALO + L, :] = z_ref[...]
    rc = CHUNK
    for c in range(L // rc):
        r0 = c * rc
        t = r0 + lax.broadcasted_iota(jnp.int32, (rc, GROUP_W), 0)
        for gi, win in enumerate(POOL_WINDOWS):
            half = win // 2
            sl = slice(gi * GROUP_W, (gi + 1) * GROUP_W)
            acc = zp_ref[POOL_HALO + r0 - half:POOL_HALO + r0 - half + rc, sl]
            for o in range(-half + 1, half):
                acc = acc + zp_ref[POOL_HALO + r0 + o:POOL_HALO + r0 + o + rc, sl]
            cnt = (jnp.minimum(t + half, L) - jnp.maximum(t - half, 0)).astype(F32)
            pooled = acc / cnt - z_ref[r0:r0 + rc, sl]
            y = jnp.dot(pooled.astype(BF16), w_ref[gi], preferred_element_type=F32)
            o_ref[r0:r0 + rc, sl] = (y * s_ref[:, sl]).astype(BF16)


def _sgu_body(u_ref, v_ref, g_ref, w_ref, b_ref, o_ref, *, L):
    for n in range(L // CHUNK):
        r0 = n * CHUNK
        v = v_ref[r0:r0 + CHUNK, :]
        vc = v - jnp.mean(v, axis=-1, keepdims=True)
        var = jnp.mean(vc * vc, axis=-1, keepdims=True)
        vn = ((vc * lax.rsqrt(var + EPS)) * g_ref[...]).astype(BF16)
        for gi in range(N_GROUPS):
            sl = slice(gi * GROUP_W, (gi + 1) * GROUP_W)
            mixed = jnp.dot(w_ref[gi], vn[:, sl], preferred_element_type=F32) + b_ref[gi]
            o_ref[r0:r0 + CHUNK, sl] = (u_ref[r0:r0 + CHUNK, sl] * mixed).astype(BF16)


def _pool_sgu_body(z_ref, u_ref, v_ref, wp_ref, s_ref, g_ref, ws_ref, b_ref, yp_ref, ys_ref, zp_ref, *, L):
    _pool_body(z_ref, wp_ref, s_ref, yp_ref, zp_ref, L=L)
    _sgu_body(u_ref, v_ref, g_ref, ws_ref, b_ref, ys_ref, L=L)


def _pool_sgu(p, L, layer, w_pool, pool_scale, ln_g, w_sp, b_sp):
    nb = p.shape[0] // L
    seq = lambda col: pl.BlockSpec((L, BRANCH_W), lambda b: (b, col))
    out = jax.ShapeDtypeStruct((p.shape[0], BRANCH_W), BF16)
    return pl.pallas_call(
        functools.partial(_pool_sgu_body, L=L),
        grid=(nb,),
        in_specs=[
            seq(S_POOL), seq(S_U), seq(S_V),
            _layer_spec(layer, N_GROUPS, GROUP_W, GROUP_W),
            _layer_spec(layer, 1, BRANCH_W),
            _layer_spec(layer, 1, BRANCH_W),
            _layer_spec(layer, N_GROUPS, CHUNK, CHUNK),
            _layer_spec(layer, N_GROUPS, CHUNK, 1),
        ],
        out_specs=[seq(0), seq(0)],
        out_shape=[out, out],
        scratch_shapes=[pltpu.VMEM((L + 2 * POOL_HALO, BRANCH_W), F32)],
        compiler_params=_params("parallel"),
        name="pool_sgu",
    )(p, p, p, w_pool, pool_scale, ln_g, w_sp, b_sp)


def _lru_scan(a_ref, b_ref, h0, nblk, reverse, emit):
    row = lax.broadcasted_iota(jnp.int32, (SUBLANES, BRANCH_W), 0)

    def body(jj, h):
        j = (nblk - 1 - jj) if reverse else jj
        r = pl.multiple_of(j * SUBLANES, SUBLANES)
        a = a_ref[pl.ds(r, SUBLANES), :]
        b = b_ref[pl.ds(r, SUBLANES), :]
        for s in (1, 2, 4):
            shift = (SUBLANES - s) if reverse else s
            a_s = pltpu.roll(a, shift, 0)
            b_s = pltpu.roll(b, shift, 0)
            m = (row < SUBLANES - s) if reverse else (row >= s)
            b = jnp.where(m, a * b_s + b, b)
            a = jnp.where(m, a * a_s, a)
        hh = a * h + b
        emit(r, hh)
        edge = hh[0:1, :] if reverse else hh[SUBLANES - 1:SUBLANES, :]
        return jnp.broadcast_to(edge, (SUBLANES, BRANCH_W))

    return lax.fori_loop(0, nblk, body, h0, unroll=4)


def _lru_body(z_ref, lg_ref, cw_ref, cb_ref, wr_ref, br_ref, wi_ref, bi_ref, lam_ref, h0_ref,
              y_ref, hl_ref, zp_ref, a_ref, b_ref, hs_ref, *, L):
    halo = jnp.zeros((CONV_HALO, BRANCH_W), F32)
    zp_ref[0:CONV_HALO, :] = halo
    zp_ref[CONV_HALO + L:2 * CONV_HALO + L, :] = halo
    zp_ref[CONV_HALO:CONV_HALO + L, :] = z_ref[...]
    left = CONV_W // 2
    rc = CHUNK
    nblk = L // SUBLANES

    def conv_chunk(r0):
        out = zp_ref[CONV_HALO + r0 - left:CONV_HALO + r0 - left + rc, :] * cw_ref[0:1, :] + cb_ref[...]
        for k in range(1, CONV_W):
            lo = CONV_HALO + r0 - left + k
            out = out + zp_ref[lo:lo + rc, :] * cw_ref[k:k + 1, :]
        return out

    def sigmoid(t):
        return 0.5 * jnp.tanh(0.5 * t) + 0.5

    sp = [jax.nn.softplus(-lam_ref[d]) for d in range(2)]
    for c in range(L // rc):
        r0 = c * rc
        cv = conv_chunk(r0)
        cvb = cv.astype(BF16)
        for d in range(2):
            for gi in range(N_GROUPS):
                sl = slice(gi * GROUP_W, (gi + 1) * GROUP_W)
                rg = sigmoid(jnp.dot(cvb[:, sl], wr_ref[d, gi], preferred_element_type=F32) + br_ref[d][:, sl])
                ig = sigmoid(jnp.dot(cvb[:, sl], wi_ref[d, gi], preferred_element_type=F32) + bi_ref[d][:, sl])
                log_a = (-LRU_C * rg) * sp[d][:, sl]
                a = jnp.exp(log_a)
                a_ref[d, r0:r0 + rc, sl] = a
                one_minus_a2 = -jnp.tanh(log_a) * (a * a + 1.0)
                b_ref[d, r0:r0 + rc, sl] = jnp.sqrt(one_minus_a2) * (ig * cv[:, sl])

    for d in range(2):
        reverse = d == 1
        h0 = jnp.broadcast_to(h0_ref[0, d:d + 1, :], (SUBLANES, BRANCH_W))
        if not reverse:
            def emit(r, hh):
                hs_ref[pl.ds(r, SUBLANES), :] = hh
        else:
            def emit(r, hh):
                tot = hs_ref[pl.ds(r, SUBLANES), :] + hh
                y_ref[pl.ds(r, SUBLANES), :] = (tot * jax.nn.gelu(lg_ref[pl.ds(r, SUBLANES), :])).astype(BF16)
        h_end = _lru_scan(a_ref.at[d], b_ref.at[d], h0, nblk, reverse, emit)
        hl_ref[0, d:d + 1, :] = h_end[0:1, :]


def _lru(p, lru_col, gate, L, layer, conv_w, conv_b, w_rg, b_rg, w_ig, b_ig, lam, h0):
    nb = p.shape[0] // L
    gate_w = _layer_spec(layer, 2, N_GROUPS, GROUP_W, GROUP_W)
    dir_vec = _layer_spec(layer, 2, 1, BRANCH_W)
    return pl.pallas_call(
        functools.partial(_lru_body, L=L),
        grid=(nb,),
        in_specs=[
            pl.BlockSpec((L, BRANCH_W), lambda b: (b, lru_col)),
            pl.BlockSpec((L, BRANCH_W), lambda b: (b, 0)),
            _layer_spec(layer, CONV_W, BRANCH_W),
            _layer_spec(layer, 1, BRANCH_W),
            gate_w, dir_vec, gate_w, dir_vec, dir_vec,
            pl.BlockSpec((1, 2, BRANCH_W), lambda b: (b, 0, 0)),
        ],
        out_specs=[
            pl.BlockSpec((L, BRANCH_W), lambda b: (b, 0)),
            pl.BlockSpec((1, 2, BRANCH_W), lambda b: (b, 0, 0)),
        ],
        out_shape=[
            jax.ShapeDtypeStruct((p.shape[0], BRANCH_W), BF16),
            jax.ShapeDtypeStruct((nb, 2, BRANCH_W), F32),
        ],
        scratch_shapes=[
            pltpu.VMEM((L + 2 * CONV_HALO, BRANCH_W), F32),
            pltpu.VMEM((2, L, BRANCH_W), F32),
            pltpu.VMEM((2, L, BRANCH_W), F32),
            pltpu.VMEM((L, BRANCH_W), F32),
        ],
        compiler_params=_params("parallel"),
        name="lru",
    )(p, gate, conv_w, conv_b, w_rg, b_rg, w_ig, b_ig, lam, h0)


_NT = (((1,), (1,)), ((), ()))


def _softmax_pv(scores, values):
    m = scores[0].max(axis=-1, keepdims=True)
    for s in scores[1:]:
        m = jnp.maximum(m, s.max(axis=-1, keepdims=True))
    es = [jnp.exp2(s - m) for s in scores]
    tot = es[0].sum(axis=-1, keepdims=True)
    for e in es[1:]:
        tot = tot + e.sum(axis=-1, keepdims=True)
    out = None
    for e, v in zip(es, values):
        o = jnp.dot(e.astype(BF16), v, preferred_element_type=F32)
        out = o if out is None else out + o
    return out * (1.0 / tot)


def _nattn_body(q_ref, k_ref, v_ref, kc_ref, vc_ref, t_ref, o_ref):
    rows = SEQ // GRID_W
    nq = NA_GROUP * GRID_W
    nk = NA_KEY_ROWS * GRID_W
    scale = GROUP_W ** -0.5 * LOG2_E
    kc = kc_ref[...]
    vc = vc_ref[...]

    def in_window(kr, row0):
        return ((kr >= row0) & (kr < row0 + NA_WIN_R)).astype(jnp.int32)

    def body(gi, carry):
        r0 = gi * NA_GROUP
        base = jnp.clip(r0 - NA_WIN_R // 2, 0, rows - NA_KEY_ROWS)
        qs = pl.multiple_of(r0 * GRID_W, nq)
        ks = pl.multiple_of(base * GRID_W, GRID_W)
        q = q_ref[pl.ds(qs, nq), :]
        kw = k_ref[pl.ds(ks, nk), :]
        vw = v_ref[pl.ds(ks, nk), :]
        s = lax.dot_general(q, kw, _NT, preferred_element_type=F32) * scale
        bias_rows = []
        for g in range(NA_GROUP):
            r = r0 + g
            row0 = jnp.clip(r - NA_WIN_R // 2, 0, rows - NA_WIN_R)
            pieces = []
            for jj in range(NA_KEY_ROWS // 2):
                kr = base + 2 * jj
                variant = in_window(kr, row0) + 2 * in_window(kr + 1, row0)
                e = jnp.clip(kr - r + NA_WIN_R, 0, NA_D - 1)
                pieces.append(t_ref[variant * NA_D + e])
            bias_rows.append(jnp.concatenate(pieces, axis=1))
        bias = jnp.concatenate(bias_rows, axis=0)
        s = jnp.where(bias > 0.5 * NEG_INF, s + bias, NEG_INF)
        sc = lax.dot_general(q, kc, _NT, preferred_element_type=F32) * scale
        o_ref[pl.ds(qs, nq), :] = _softmax_pv([s, sc], [vw, vc]).astype(BF16)
        return carry

    lax.fori_loop(0, rows // NA_GROUP, body, 0, unroll=2)


def _nattn(qkv, qkv_c, layer, bias_table):
    return pl.pallas_call(
        _nattn_body,
        grid=(BATCH, N_GROUPS),
        in_specs=[
            pl.BlockSpec((SEQ, GROUP_W), lambda b, h: (b, Q_COL + h)),
            pl.BlockSpec((SEQ, GROUP_W), lambda b, h: (b, K_COL + h)),
            pl.BlockSpec((SEQ, GROUP_W), lambda b, h: (b, V_COL + h)),
            pl.BlockSpec((CTX_LEN, GROUP_W), lambda b, h: (b, K_COL + h)),
            pl.BlockSpec((CTX_LEN, GROUP_W), lambda b, h: (b, V_COL + h)),
            pl.BlockSpec((None, None, 4 * NA_D, GRID_W, 2 * GRID_W), lambda b, h: (layer, h, 0, 0, 0)),
        ],
        out_specs=pl.BlockSpec((SEQ, GROUP_W), lambda b, h: (b, h)),
        out_shape=jax.ShapeDtypeStruct((qkv.shape[0], BRANCH_W), BF16),
        compiler_params=_params("parallel", "parallel"),
        name="nattn",
    )(qkv, qkv, qkv, qkv_c, qkv_c, bias_table)


def _cattn_body(q_ref, k_ref, v_ref, o_ref):
    scale = GROUP_W ** -0.5 * LOG2_E
    s = lax.dot_general(q_ref[...], k_ref[...], _NT, preferred_element_type=F32) * scale
    o_ref[...] = _softmax_pv([s], [v_ref[...]]).astype(BF16)


def _cattn(qkv_c):
    return pl.pallas_call(
        _cattn_body,
        grid=(BATCH, N_GROUPS),
        in_specs=[
            pl.BlockSpec((CTX_LEN, GROUP_W), lambda b, h: (b, Q_COL + h)),
            pl.BlockSpec((CTX_LEN, GROUP_W), lambda b, h: (b, K_COL + h)),
            pl.BlockSpec((CTX_LEN, GROUP_W), lambda b, h: (b, V_COL + h)),
        ],
        out_specs=pl.BlockSpec((CTX_LEN, GROUP_W), lambda b, h: (b, h)),
        out_shape=jax.ShapeDtypeStruct((qkv_c.shape[0], BRANCH_W), BF16),
        compiler_params=_params("parallel", "parallel"),
        name="cattn",
    )(qkv_c, qkv_c, qkv_c)


def _bias_table(rpb):
    col = jnp.arange(GRID_W)
    col_start = jnp.clip(col - NA_WIN_C // 2, 0, GRID_W - NA_WIN_C)
    col_ok = (col[None, :] >= col_start[:, None]) & (col[None, :] < col_start[:, None] + NA_WIN_C)
    pad = GRID_W - NA_WIN_C
    period = 2 * GRID_W - 1
    u = jnp.pad(rpb.astype(F32), [(0, 0)] * 3 + [(pad, pad)])
    skew = jnp.tile(u, GRID_W + 1)[..., :GRID_W * (period + 1)]
    skew = skew.reshape(rpb.shape[:3] + (GRID_W, period + 1))
    toe = skew[..., ::-1, :GRID_W]
    tab = jnp.where(col_ok, toe * LOG2_E, NEG_INF)
    off = jnp.full_like(tab[:, :, :1], NEG_INF)
    lo = jnp.concatenate([off, tab], axis=2)
    hi = jnp.concatenate([tab, off], axis=2)
    none = jnp.full_like(lo, NEG_INF)
    variants = [jnp.concatenate([lo if v & 1 else none, hi if v & 2 else none], axis=-1) for v in range(4)]
    return jnp.concatenate(variants, axis=2)


def _merge_body(h_ref, y0_ref, y1_ref, y2_ref, y3_ref, g0_ref, g1_ref, g2_ref, g3_ref, wb_ref, o_ref):
    h = h_ref[...]
    acc = None
    for n, (y_ref, g_ref) in enumerate(((y0_ref, g0_ref), (y1_ref, g1_ref), (y2_ref, g2_ref), (y3_ref, g3_ref))):
        gate = jax.nn.sigmoid(jnp.dot(h, g_ref[...], preferred_element_type=F32))
        term = gate * jnp.dot(y_ref[...], wb_ref[n], preferred_element_type=F32)
        acc = term if acc is None else acc + term
    o_ref[...] = acc.astype(BF16)


def _merge(h, ys, layer, w_gate, w_branch):
    rows = h.shape[0]
    tm, tn = TM_MERGE, 512
    nj = D_MODEL // tn
    gate_spec = lambda n: pl.BlockSpec((D_MODEL, tn), lambda i, j: (0, n * nj + j))
    y_spec = pl.BlockSpec((tm, BRANCH_W), lambda i, j: (i, 0))
    return pl.pallas_call(
        _merge_body,
        grid=(rows // tm, nj),
        in_specs=[pl.BlockSpec((tm, D_MODEL), lambda i, j: (i, 0)), y_spec, y_spec, y_spec, y_spec,
                  gate_spec(0), gate_spec(1), gate_spec(2), gate_spec(3),
                  pl.BlockSpec((None, N_BRANCH, BRANCH_W, tn), lambda i, j: (layer, 0, 0, j))],
        out_specs=pl.BlockSpec((tm, tn), lambda i, j: (i, j)),
        out_shape=jax.ShapeDtypeStruct((rows, D_MODEL), BF16),
        compiler_params=_params("parallel", "arbitrary"),
        name="merge",
    )(h, *ys, w_gate, w_gate, w_gate, w_gate, w_branch)


def _outproj_body(m_ref, x_ref, w_ref, gt_ref, g_ref, sh_ref, sc_ref, xm_ref, h2_ref, gs_ref, *, tm):
    xm_ref[...] = x_ref[...] + gt_ref[...] * jnp.dot(m_ref[...], w_ref[...], preferred_element_type=F32)
    _rms_mod_rows(xm_ref, h2_ref, g_ref, sc_ref, sh_ref, gs_ref, tm)


def _outproj(merged, x, mods, layer, row_of_tile, w_out, g_ffn):
    rows = x.shape[0]
    tm = TM_OUT
    row_spec = pl.BlockSpec((tm, D_MODEL), lambda i: (i, 0))
    return pl.pallas_call(
        functools.partial(_outproj_body, tm=tm),
        grid=(rows // tm,),
        in_specs=[row_spec, row_spec,
                  _layer_spec(layer, D_MODEL, D_MODEL),
                  _mod_spec(layer, 2, row_of_tile(tm)),
                  _layer_spec(layer, 1, D_MODEL),
                  _mod_spec(layer, 3, row_of_tile(tm)),
                  _mod_spec(layer, 4, row_of_tile(tm))],
        out_specs=[row_spec, row_spec],
        out_shape=[jax.ShapeDtypeStruct((rows, D_MODEL), F32),
                   jax.ShapeDtypeStruct((rows, D_MODEL), BF16)],
        scratch_shapes=[pltpu.VMEM((1, D_MODEL), F32)],
        compiler_params=_params("parallel"),
        name="outproj",
    )(merged, x, w_out, mods, g_ffn, mods, mods)


def _ffn_body(h_ref, wg_ref, wu_ref, wd_ref, xm_ref, gt_ref, gf_ref, o_ref, a_ref, rs_ref, *, tm, tf, final):
    f = pl.program_id(1)

    @pl.when(f == 0)
    def _():
        o_ref[...] = jnp.zeros_like(o_ref)

    h = h_ref[...]
    for s in range(tf // FFN_SUB):
        sl = slice(s * FFN_SUB, (s + 1) * FFN_SUB)
        g = jnp.dot(h, wg_ref[:, sl], preferred_element_type=F32)
        u = jnp.dot(h, wu_ref[:, sl], preferred_element_type=F32)
        a_ref[:, sl] = (jax.nn.silu(g) * u).astype(BF16)
        o_ref[...] += jnp.dot(a_ref[:, sl], wd_ref[sl, :], preferred_element_type=F32)

    @pl.when(f == pl.num_programs(1) - 1)
    def _():
        def residual(c, carry):
            r = pl.multiple_of(c * NORM_ROWS, NORM_ROWS)
            y = xm_ref[pl.ds(r, NORM_ROWS), :] + gt_ref[...] * o_ref[pl.ds(r, NORM_ROWS), :]
            o_ref[pl.ds(r, NORM_ROWS), :] = y
            if final:
                ms = jnp.mean(y * y, axis=-1, keepdims=True)
                rs_ref[pl.ds(r, NORM_ROWS), :] = jnp.broadcast_to(lax.rsqrt(ms + EPS), (NORM_ROWS, LANES))
            return carry
        lax.fori_loop(0, tm // NORM_ROWS, residual, 0, unroll=4)

        if final:
            def scale(c, carry):
                r = pl.multiple_of(c * NORM_ROWS, NORM_ROWS)
                rs = jnp.concatenate([rs_ref[pl.ds(r, NORM_ROWS), :]] * (D_MODEL // LANES), axis=1)
                o_ref[pl.ds(r, NORM_ROWS), :] = (o_ref[pl.ds(r, NORM_ROWS), :] * rs) * gf_ref[...]
                return carry
            lax.fori_loop(0, tm // NORM_ROWS, scale, 0, unroll=4)


def _ffn(h2, xm, mods, layer, row_of_tile, w_g, w_u, w_d, g_final, final):
    rows = xm.shape[0]
    tm, tf = TM_FFN, 512
    row_spec = pl.BlockSpec((tm, D_MODEL), lambda i, f: (i, 0))
    return pl.pallas_call(
        functools.partial(_ffn_body, tm=tm, tf=tf, final=final),
        grid=(rows // tm, D_FF // tf),
        in_specs=[row_spec,
                  pl.BlockSpec((None, D_MODEL, tf), lambda i, f: (layer, 0, f)),
                  pl.BlockSpec((None, D_MODEL, tf), lambda i, f: (layer, 0, f)),
                  pl.BlockSpec((None, tf, D_MODEL), lambda i, f: (layer, f, 0)),
                  row_spec,
                  _mod_spec(layer, 5, row_of_tile(tm)),
                  pl.BlockSpec((1, D_MODEL), lambda i, f: (0, 0))],
        out_specs=row_spec,
        out_shape=jax.ShapeDtypeStruct((rows, D_MODEL), F32),
        scratch_shapes=[pltpu.VMEM((tm, tf), BF16), pltpu.VMEM((tm, LANES), F32)],
        compiler_params=_params("parallel", "arbitrary"),
        name="ffn",
    )(h2, w_g, w_u, w_d, xm, mods, g_final)


def kernel(x, c, ctx, c_ctx, w_mod, b_mod, g_mix, g_ffn, g_final, w_in, w_pool, pool_scale, gmlp_ln_g, w_sp, b_sp, conv_w, conv_b, w_rg, b_rg, w_ig, b_ig, lru_lam, rpb, w_branch, w_gate, w_out, w_ffn_gate, w_ffn_up, w_ffn_down):
    B, L, D = x.shape
    C = ctx.shape[1]
    assert (B, L, D, C) == (BATCH, SEQ, D_MODEL, CTX_LEN)
    xs = x.reshape(B * L, D)
    cs = ctx.reshape(B * C, D)

    cond = jnp.concatenate([c, c_ctx[None, :], jnp.zeros((COND_ROWS - B - 1, D), F32)], axis=0)
    mods = _adaln_all(cond, w_mod, b_mod)

    w_in_b = w_in.astype(BF16)
    w_gate_b = None
    w_branch_b = w_branch.astype(BF16)
    w_out_b = w_out.astype(BF16)
    w_fg, w_fu, w_fd = w_ffn_gate.astype(BF16), w_ffn_up.astype(BF16), w_ffn_down.astype(BF16)
    g_mix_r = g_mix.reshape(DEPTH, 1, D)
    g_ffn_r = g_ffn.reshape(DEPTH, 1, D)
    g_fin_r = g_final.reshape(1, D)
    pool_args = (w_pool.astype(BF16), pool_scale.reshape(DEPTH, 1, BRANCH_W))
    sgu_args = (gmlp_ln_g.reshape(DEPTH, 1, BRANCH_W), w_sp.astype(BF16), b_sp.reshape(DEPTH, N_GROUPS, CHUNK, 1))
    lru_args = (conv_w, conv_b.reshape(DEPTH, 1, BRANCH_W),
                w_rg.astype(BF16), b_rg.reshape(DEPTH, 2, 1, BRANCH_W),
                w_ig.astype(BF16), b_ig.reshape(DEPTH, 2, 1, BRANCH_W),
                lru_lam.reshape(DEPTH, 2, 1, BRANCH_W))
    bias_table = _bias_table(rpb)
    h_zero = jnp.zeros((B, 2, BRANCH_W), F32)

    for l in range(DEPTH):
        last = l == DEPTH - 1
        px, gx, qkv, hx = _inproj(xs, mods, l, _latent_row, g_mix_r, w_in_b)
        if w_gate_b is None:
            hx, w_gate = lax.optimization_barrier((hx, w_gate))
            w_gate_b = [w_gate[k].astype(BF16).reshape(D, N_BRANCH * D) for k in range(DEPTH)]
        ctx_first = 1 if last else 0
        pc, gc, qkv_c, hc = _inproj(cs, mods, l, _context_row, g_mix_r, w_in_b, ctx_first)

        yc_lru, hc_state = _lru(pc, S_LRU - 2 * ctx_first, gc, C, l, *lru_args, h_zero)

        y_pool, y_sgu = _pool_sgu(px, L, l, *pool_args, *sgu_args)
        y_lru, _ = _lru(px, S_LRU, gx, L, l, *lru_args, hc_state)
        y_na = _nattn(qkv, qkv_c, l, bias_table)
        merged = _merge(hx, (y_pool, y_sgu, y_lru, y_na), l, w_gate_b[l], w_branch_b)
        xm, hx2 = _outproj(merged, xs, mods, l, _latent_row, w_out_b, g_ffn_r)

        if not last:
            yc_pool, yc_sgu = _pool_sgu(pc, C, l, *pool_args, *sgu_args)
            yc_na = _cattn(qkv_c)
            merged_c = _merge(hc, (yc_pool, yc_sgu, yc_lru, yc_na), l, w_gate_b[l], w_branch_b)
            cm, hc2 = _outproj(merged_c, cs, mods, l, _context_row, w_out_b, g_ffn_r)
            cs = _ffn(hc2, cm, mods, l, _context_row, w_fg, w_fu, w_fd, g_fin_r, False)

        xs = _ffn(hx2, xm, mods, l, _latent_row, w_fg, w_fu, w_fd, g_fin_r, last)

    return xs.reshape(B, L, D)
```
